```python
import math
import jax, jax.numpy as jnp
from jax import lax
import numpy as np

D_MODEL = 1024
BATCH = 2
SEQ = 8192
DEPTH = 1
DEC_BATCH = 32
DEC_SEQ = 8
PAST_LEN = 16384
PAGE_SIZE = 128

N_HEADS = 8
HEAD_DIM = 64
ATTN_WIDTH = N_HEADS * HEAD_DIM
N_IDX_HEADS = 8
IDX_DIM = 64
TOPK_MAX = 256
ROPE_THETA = 500000.0
ROT_DIM = HEAD_DIM // 4
D_CONV = D_MODEL // 2
CONV_WIDTH = 31
N_BRANCHES = 2
N_EXPERTS = 32
TOP_K = 4
D_FF = D_MODEL
SWIGLU_LIMIT = 7.0
SWIGLU_ALPHA = 1.702
NORM_EPS = 1e-5
Q_BLOCK = 128
D_IN = 3 * ATTN_WIDTH + N_IDX_HEADS * IDX_DIM + IDX_DIM + N_IDX_HEADS + 2 * D_CONV + N_BRANCHES * D_MODEL

kernel_name = 'dsa_conformer_gated_moe_step'


def _split_points():
    sizes = (ATTN_WIDTH, ATTN_WIDTH, ATTN_WIDTH, N_IDX_HEADS * IDX_DIM, IDX_DIM, N_IDX_HEADS, 2 * D_CONV)
    pts, acc = [], 0
    for s in sizes:
        acc += s
        pts.append(acc)
    return pts


def rms_norm(x, g):
    xf = x.astype(jnp.float32)
    y = xf * lax.rsqrt(jnp.mean(xf * xf, axis=-1, keepdims=True) + NORM_EPS)
    return (y * g.astype(jnp.float32)).astype(x.dtype)


def layer_norm(x, g, b):
    xf = x.astype(jnp.float32)
    mu = jnp.mean(xf, axis=-1, keepdims=True)
    var = jnp.mean(jnp.square(xf - mu), axis=-1, keepdims=True)
    y = (xf - mu) * lax.rsqrt(var + NORM_EPS)
    return (y * g.astype(jnp.float32) + b.astype(jnp.float32)).astype(x.dtype)


def partial_rope(x, pos):
    half = ROT_DIM // 2
    inv_freq = jnp.exp(-math.log(ROPE_THETA) * jnp.arange(half, dtype=jnp.float32) * (2.0 / ROT_DIM))
    ang = pos.astype(jnp.float32)[:, None] * inv_freq[None, :]
    cos = jnp.cos(ang)[None, :, None, :]
    sin = jnp.sin(ang)[None, :, None, :]
    xr = x[..., :ROT_DIM].astype(jnp.float32)
    x1, x2 = xr[..., :half], xr[..., half:]
    rot = jnp.concatenate([x1 * cos - x2 * sin, x2 * cos + x1 * sin], axis=-1).astype(x.dtype)
    return jnp.concatenate([rot, x[..., ROT_DIM:]], axis=-1)


def in_projection(xn, w_in, pos):
    b, t, _ = xn.shape
    h = xn @ w_in
    q, k, v, qi, ki, wi, cu, gt = jnp.split(h, _split_points(), axis=-1)
    q = partial_rope(q.reshape(b, t, N_HEADS, HEAD_DIM), pos)
    k = partial_rope(k.reshape(b, t, N_HEADS, HEAD_DIM), pos)
    v = v.reshape(b, t, N_HEADS, HEAD_DIM)
    qi = partial_rope(qi.reshape(b, t, N_IDX_HEADS, IDX_DIM), pos)
    ki = partial_rope(ki[:, :, None, :], pos)[:, :, 0, :]
    gt = gt.reshape(b, t, N_BRANCHES, D_MODEL)
    return q, k, v, qi, ki, wi, cu, gt


def gather_rows(x, idx):
    return jax.vmap(lambda xb, ib: xb[ib])(x, idx)


def indexer_topk(qi, wi, ki, t_pos, s_pos, n_keep):
    dots = jnp.einsum('bqhd,bsd->bqsh', qi, ki).astype(jnp.float32) * (IDX_DIM ** -0.5)
    score = jnp.einsum('bqsh,bqh->bqs', jax.nn.relu(dots), wi.astype(jnp.float32)) * (N_IDX_HEADS ** -0.5)
    visible = s_pos[None, :] <= t_pos[:, None]
    score = jnp.where(visible[None], score, -jnp.inf)
    _, idx = lax.top_k(score, n_keep)
    valid = idx <= t_pos[None, :, None]
    return idx, valid


def sparse_attend(q, k_sel, v_sel, valid):
    logits = jnp.einsum('bqhd,bqkhd->bqhk', q, k_sel).astype(jnp.float32) * (HEAD_DIM ** -0.5)
    logits = jnp.where(valid[:, :, None, :], logits, -jnp.inf)
    p = jax.nn.softmax(logits, axis=-1).astype(v_sel.dtype)
    return jnp.einsum('bqhk,bqkhd->bqhd', p, v_sel)


def prompt_attention(q, k, v, qi, wi, ki):
    b, t = q.shape[:2]
    n_keep = min(TOPK_MAX, t // 4)
    nb = t // Q_BLOCK
    s_pos = jnp.arange(t, dtype=jnp.int32)

    def to_blocks(a):
        return jnp.swapaxes(a.reshape(b, nb, Q_BLOCK, *a.shape[2:]), 0, 1)

    def block(args):
        blk, qb, qib, wib = args
        t_pos = blk * Q_BLOCK + jnp.arange(Q_BLOCK, dtype=jnp.int32)
        idx, valid = indexer_topk(qib, wib, ki, t_pos, s_pos, n_keep)
        return sparse_attend(qb, gather_rows(k, idx), gather_rows(v, idx), valid)

    o = lax.map(block, (jnp.arange(nb, dtype=jnp.int32), to_blocks(q), to_blocks(qi), to_blocks(wi)))
    return jnp.swapaxes(o, 0, 1).reshape(b, t, ATTN_WIDTH)


def sample_attention(q, k, v, qi, wi, ki, cache_k, cache_v, cache_kidx, page_table):
    db, ds = q.shape[:2]
    past = page_table.shape[1] * PAGE_SIZE
    n_keep = min(TOPK_MAX, (past + ds) // 4)
    ki_past = cache_kidx[page_table].reshape(db, past, IDX_DIM)
    ki_all = jnp.concatenate([ki_past, ki], axis=1)
    t_pos = past + jnp.arange(ds, dtype=jnp.int32)
    s_pos = jnp.arange(past + ds, dtype=jnp.int32)
    idx, valid = indexer_topk(qi, wi, ki_all, t_pos, s_pos, n_keep)
    in_past = (idx < past)[..., None, None]
    pidx = jnp.minimum(idx, past - 1)
    phys = jax.vmap(lambda pt, i: pt[i])(page_table, pidx // PAGE_SIZE)
    off = pidx % PAGE_SIZE
    nidx = jnp.clip(idx - past, 0, ds - 1)
    k_sel = jnp.where(in_past, cache_k[phys, off], gather_rows(k, nidx))
    v_sel = jnp.where(in_past, cache_v[phys, off], gather_rows(v, nidx))
    return sparse_attend(q, k_sel, v_sel, valid).reshape(db, ds, ATTN_WIDTH)


def conv_branch(cu, prev, w_dw, b_dw, ln_g, ln_b, w_conv_out, b_conv_out):
    glu = cu[..., :D_CONV] * jax.nn.sigmoid(cu[..., D_CONV:])
    padded = jnp.concatenate([prev.astype(glu.dtype), glu], axis=1)
    dw = lax.conv_general_dilated(padded, w_dw[:, None, :].astype(glu.dtype), (1,), 'VALID',
                                  dimension_numbers=('NWC', 'WIO', 'NWC'),
                                  feature_group_count=D_CONV) + b_dw
    h = jax.nn.silu(layer_norm(dw, ln_g, ln_b))
    y = h @ w_conv_out + b_conv_out
    return y, padded[:, -(CONV_WIDTH - 1):]


def merge_branches(o_attn, y_conv, gt, w_attn_out, w_mix_out):
    y_attn = o_attn @ w_attn_out
    g = jax.nn.sigmoid(gt.astype(jnp.float32)).astype(y_attn.dtype)
    return (g[..., 0, :] * y_attn + g[..., 1, :] * y_conv) @ w_mix_out


def moe_ffn(x, g_norm, w_router, b_router, w_gate, b_gate, w_up, b_up, w_down, b_down):
    xn = rms_norm(x, g_norm)
    logits = (xn @ w_router).astype(jnp.float32) + b_router.astype(jnp.float32)
    top_v, top_i = lax.top_k(logits, TOP_K)
    top_w = jax.nn.softmax(top_v, axis=-1)
    comb = jnp.sum(jax.nn.one_hot(top_i, N_EXPERTS, dtype=jnp.float32) * top_w[..., None], axis=1).astype(x.dtype)
    out = jnp.zeros_like(x)
    for e in range(N_EXPERTS):
        gate = jnp.minimum(xn @ w_gate[e] + b_gate[e], SWIGLU_LIMIT)
        up = jnp.clip(xn @ w_up[e] + b_up[e], -SWIGLU_LIMIT, SWIGLU_LIMIT)
        act = (up + 1.0) * gate * jax.nn.sigmoid(SWIGLU_ALPHA * gate)
        out = out + comb[:, e:e + 1] * (act @ w_down[e] + b_down[e])
    return out


def setup_inputs(seed: int = 0) -> dict:
    key = jax.random.key(seed)
    ks = jax.random.split(key, 28)
    f32 = jnp.float32
    n_pages = PAST_LEN // PAGE_SIZE
    n_pool = (DEC_BATCH * n_pages * 5) // 4
    nrm = lambda k, s, sc: jax.random.normal(k, s, f32) * sc
    page_table = jax.random.permutation(ks[6], n_pool)[:DEC_BATCH * n_pages].reshape(DEC_BATCH, n_pages).astype(jnp.int32)
    return {
        'x_prompt': nrm(ks[0], (BATCH, SEQ, D_MODEL), 1.0),
        'x_sample': nrm(ks[1], (DEC_BATCH, DEC_SEQ, D_MODEL), 1.0),
        'cache_k': nrm(ks[2], (DEPTH, n_pool, PAGE_SIZE, N_HEADS, HEAD_DIM), 1.0),
        'cache_v': nrm(ks[3], (DEPTH, n_pool, PAGE_SIZE, N_HEADS, HEAD_DIM), 1.0),
        'cache_kidx': nrm(ks[4], (DEPTH, n_pool, PAGE_SIZE, IDX_DIM), 1.0),
        'state_conv': nrm(ks[5], (DEPTH, DEC_BATCH, CONV_WIDTH - 1, D_CONV), 1.0),
        'page_table': page_table,
        'g_mix': 1.0 + nrm(ks[7], (DEPTH, D_MODEL), 0.02),
        'w_in': nrm(ks[8], (DEPTH, D_MODEL, D_IN), D_MODEL ** -0.5),
        'w_attn_out': nrm(ks[9], (DEPTH, ATTN_WIDTH, D_MODEL), ATTN_WIDTH ** -0.5),
        'w_dw': nrm(ks[10], (DEPTH, CONV_WIDTH, D_CONV), CONV_WIDTH ** -0.5),
        'b_dw': nrm(ks[11], (DEPTH, D_CONV), 0.01),
        'ln_g': 1.0 + nrm(ks[12], (DEPTH, D_CONV), 0.02),
        'ln_b': nrm(ks[13], (DEPTH, D_CONV), 0.01),
        'w_conv_out': nrm(ks[14], (DEPTH, D_CONV, D_MODEL), D_CONV ** -0.5),
        'b_conv_out': nrm(ks[15], (DEPTH, D_MODEL), 0.01),
        'w_mix_out': nrm(ks[16], (DEPTH, D_MODEL, D_MODEL), D_MODEL ** -0.5),
        'g_ffn': 1.0 + nrm(ks[17], (DEPTH, D_MODEL), 0.02),
        'w_router': nrm(ks[18], (DEPTH, D_MODEL, N_EXPERTS), D_MODEL ** -0.5),
        'b_router': nrm(ks[19], (DEPTH, N_EXPERTS), 0.01),
        'w_gate': nrm(ks[20], (DEPTH, N_EXPERTS, D_MODEL, D_FF), D_MODEL ** -0.5),
        'b_gate': nrm(ks[21], (DEPTH, N_EXPERTS, D_FF), 0.01),
        'w_up': nrm(ks[22], (DEPTH, N_EXPERTS, D_MODEL, D_FF), D_MODEL ** -0.5),
        'b_up': nrm(ks[23], (DEPTH, N_EXPERTS, D_FF), 0.01),
        'w_down': nrm(ks[24], (DEPTH, N_EXPERTS, D_FF, D_MODEL), D_FF ** -0.5),
        'b_down': nrm(ks[25], (DEPTH, N_EXPERTS, D_MODEL), 0.01),
        'g_final': 1.0 + nrm(ks[26], (D_MODEL,), 0.02),
    }


def reference(x_prompt, x_sample, cache_k, cache_v, cache_kidx, state_conv, page_table,
              g_mix, w_in, w_attn_out, w_dw, b_dw, ln_g, ln_b, w_conv_out, b_conv_out, w_mix_out,
              g_ffn, w_router, b_router, w_gate, b_gate, w_up, b_up, w_down, b_down, g_final):
    b, t, _ = x_prompt.shape
    db, ds, _ = x_sample.shape
    past = page_table.shape[1] * PAGE_SIZE
    pos_p = jnp.arange(t, dtype=jnp.int32)
    pos_s = past + jnp.arange(ds, dtype=jnp.int32)
    xp, xs = x_prompt, x_sample
    kp, vp, kip, cp, ks_, vs_, kis, cs = [], [], [], [], [], [], [], []
    for l in range(DEPTH):
        q, k, v, qi, ki, wi, cu, gt = in_projection(rms_norm(xp, g_mix[l]), w_in[l], pos_p)
        o = prompt_attention(q, k, v, qi, wi, ki)
        yc, conv_p = conv_branch(cu, jnp.zeros((b, CONV_WIDTH - 1, D_CONV), cu.dtype),
                                 w_dw[l], b_dw[l], ln_g[l], ln_b[l], w_conv_out[l], b_conv_out[l])
        xp = xp + merge_branches(o, yc, gt, w_attn_out[l], w_mix_out[l])
        kp.append(k); vp.append(v); kip.append(ki); cp.append(conv_p)
        q, k, v, qi, ki, wi, cu, gt = in_projection(rms_norm(xs, g_mix[l]), w_in[l], pos_s)
        o = sample_attention(q, k, v, qi, wi, ki, cache_k[l], cache_v[l], cache_kidx[l], page_table)
        yc, conv_s = conv_branch(cu, state_conv[l],
                                 w_dw[l], b_dw[l], ln_g[l], ln_b[l], w_conv_out[l], b_conv_out[l])
        xs = xs + merge_branches(o, yc, gt, w_attn_out[l], w_mix_out[l])
        ks_.append(k); vs_.append(v); kis.append(ki); cs.append(conv_s)
        flat = jnp.concatenate([xp.reshape(b * t, D_MODEL), xs.reshape(db * ds, D_MODEL)], axis=0)
        flat = flat + moe_ffn(flat, g_ffn[l], w_router[l], b_router[l], w_gate[l], b_gate[l],
                              w_up[l], b_up[l], w_down[l], b_down[l])
        xp = flat[:b * t].reshape(b, t, D_MODEL)
        xs = flat[b * t:].reshape(db, ds, D_MODEL)
    y_prompt = rms_norm(xp, g_final)
    y_sample = rms_norm(xs, g_final)
    return (y_prompt, y_sample, jnp.stack(kp), jnp.stack(vp), jnp.stack(kip), jnp.stack(cp),
            jnp.stack(ks_), jnp.stack(vs_), jnp.stack(kis), jnp.stack(cs))
```

```python
import functools
import math

import jax
import jax.numpy as jnp
from jax import lax
from jax.experimental import pallas as pl
from jax.experimental.pallas import tpu as pltpu

N_HEADS = 8
HEAD_DIM = 64
ATTN_WIDTH = N_HEADS * HEAD_DIM
N_IDX_HEADS = 8
IDX_DIM = 64
TOPK_MAX = 256
ROPE_THETA = 500000.0
ROT_DIM = HEAD_DIM // 4
CONV_WIDTH = 31
N_BRANCHES = 2
TOP_K = 4
SWIGLU_LIMIT = 7.0
SWIGLU_ALPHA = 1.702
NORM_EPS = 1e-5
PAGE_SIZE = 128
Q_BLOCK = 128

LANES = 128
SUBLANES = 8
VMEM_LIMIT = 56 * 1024 * 1024
INT_MIN = -(2 ** 31)
NEG_BIG = -1e30
QK_SCALE = HEAD_DIM ** -0.5
IDX_QK_SCALE = IDX_DIM ** -0.5
IDX_HEAD_SCALE = N_IDX_HEADS ** -0.5

F32 = jnp.float32
BF16 = jnp.bfloat16
I32 = jnp.int32


def _cparams(sem):
    return pltpu.CompilerParams(dimension_semantics=sem, vmem_limit_bytes=VMEM_LIMIT)


def _dot(a, b):
    return jnp.dot(a, b, preferred_element_type=F32)


def _dot_nt(a, b):
    return lax.dot_general(a, b, (((1,), (1,)), ((), ())), preferred_element_type=F32)


def _rms(x, g):
    return x * lax.rsqrt(jnp.mean(x * x, axis=-1, keepdims=True) + NORM_EPS) * g


def _rope(seg, c, s1, s2):
    outs = []
    for j in range(seg.shape[1] // LANES):
        blk = seg[:, j * LANES:(j + 1) * LANES]
        outs.append(blk * c + pltpu.roll(blk, LANES - ROT_DIM // 2, 1) * s1
                    + pltpu.roll(blk, ROT_DIM // 2, 1) * s2)
    return outs[0] if len(outs) == 1 else jnp.concatenate(outs, axis=1)


def _to_key(score):
    bits = lax.bitcast_convert_type(score, I32)
    key = jnp.where(bits < 0, bits ^ jnp.int32(0x7FFFFFFF), bits)
    return jnp.where(key == -1, 0, key)


def _fold_lanes(m):
    acc = m[:, 0:LANES]
    for j in range(1, m.shape[1] // LANES):
        acc = acc + m[:, j * LANES:(j + 1) * LANES]
    return acc


def _inproj_kernel(x_ref, g_ref, wm_ref, ws_ref, c_ref, s1_ref, s2_ref,
                   q_ref, kf_ref, kb_ref, vf_ref, vb_ref, qi_ref, kif_ref, kib_ref, wi_ref):
    xn = _rms(x_ref[...], g_ref[...]).astype(BF16)
    c, s1, s2 = c_ref[...], s1_ref[...], s2_ref[...]
    aw = ATTN_WIDTH
    h = _dot(xn, wm_ref[...])
    q_ref[...] = (_rope(h[:, 0:aw], c, s1, s2) * QK_SCALE).astype(BF16)
    k = _rope(h[:, aw:2 * aw], c, s1, s2)
    kf_ref[...] = k
    kb_ref[...] = k.astype(BF16)
    v = h[:, 2 * aw:3 * aw]
    vf_ref[...] = v
    vb_ref[...] = v.astype(BF16)
    qi_ref[...] = (_rope(h[:, 3 * aw:4 * aw], c, s1, s2) * IDX_QK_SCALE).astype(BF16)
    hs = _dot(xn, ws_ref[...])
    ki = _rope(hs, c, s1, s2)[:, 0:IDX_DIM]
    kif_ref[...] = ki
    kib_ref[...] = ki.astype(BF16)
    wi_ref[...] = hs[:, IDX_DIM:IDX_DIM + N_IDX_HEADS]


def _in_projection(x2d, g_mix, w_main, w_small, tabs, tm):
    n, d = x2d.shape
    aw = ATTN_WIDTH
    row = lambda w: pl.BlockSpec((tm, w), lambda i: (i, 0))
    const = lambda a: pl.BlockSpec(a.shape, lambda i: (0, 0))
    out_shapes = (
        jax.ShapeDtypeStruct((n, aw), BF16),
        jax.ShapeDtypeStruct((n, aw), F32),
        jax.ShapeDtypeStruct((n, aw), BF16),
        jax.ShapeDtypeStruct((n, aw), F32),
        jax.ShapeDtypeStruct((n, aw), BF16),
        jax.ShapeDtypeStruct((n, aw), BF16),
        jax.ShapeDtypeStruct((n, IDX_DIM), F32),
        jax.ShapeDtypeStruct((n, IDX_DIM), BF16),
        jax.ShapeDtypeStruct((n, N_IDX_HEADS), F32),
    )
    out_specs = (row(aw), row(aw), row(aw), row(aw), row(aw), row(aw),
                 row(IDX_DIM), row(IDX_DIM), row(N_IDX_HEADS))
    return pl.pallas_call(
        _inproj_kernel,
        grid=(n // tm,),
        in_specs=[row(d), const(g_mix), const(w_main), const(w_small),
                  row(LANES), row(LANES), row(LANES)],
        out_specs=out_specs,
        out_shape=out_shapes,
        compiler_params=_cparams(("arbitrary",)),
        name="in_projection",
    )(x2d, g_mix, w_main, w_small, *tabs)


HIST = 32


def _conv_tail(dw, bdw, lng, lnb, wco, bco):
    dw = dw + bdw
    mu = jnp.mean(dw, axis=-1, keepdims=True)
    cen = dw - mu
    var = jnp.mean(cen * cen, axis=-1, keepdims=True)
    y = cen * lax.rsqrt(var + NORM_EPS) * lng + lnb
    hh = y * jax.nn.sigmoid(y)
    return _dot(hh.astype(BF16), wco) + bco


def _depthwise(pad_ref, wdw_ref, rows):
    base = HIST - (CONV_WIDTH - 1)
    acc = pad_ref[pl.ds(base, rows), :] * wdw_ref[0:1, :]
    for j in range(1, CONV_WIDTH):
        acc = acc + pad_ref[pl.ds(base + j, rows), :] * wdw_ref[j:j + 1, :]
    return acc


def _conv_prompt_kernel(x_ref, g_ref, wcu_ref, wdw_ref, bdw_ref, lng_ref, lnb_ref, wco_ref, bco_ref,
                        y_ref, st_ref, pad_ref):
    t = pl.program_id(1)
    tm = x_ref.shape[0]
    dc = wdw_ref.shape[1]

    @pl.when(t == 0)
    def _():
        pad_ref[0:HIST, :] = jnp.zeros((HIST, dc), F32)

    xn = _rms(x_ref[...], g_ref[...]).astype(BF16)
    cu = _dot(xn, wcu_ref[...])
    glu = cu[:, 0:dc] * jax.nn.sigmoid(cu[:, dc:2 * dc])
    pad_ref[HIST:HIST + tm, :] = glu
    dw = _depthwise(pad_ref, wdw_ref, tm)
    y_ref[...] = _conv_tail(dw, bdw_ref[...], lng_ref[...], lnb_ref[...], wco_ref[...], bco_ref[...])
    tail = pad_ref[tm:tm + HIST, :]
    pad_ref[0:HIST, :] = tail

    @pl.when(t == pl.num_programs(1) - 1)
    def _():
        st_ref[...] = tail[HIST - (CONV_WIDTH - 1):, :]


def _conv_prompt(x2d, nb, g_mix, wcu, wdw, bdw, lng, lnb, wco, bco, tm):
    n, d = x2d.shape
    t = n // nb
    dc = wdw.shape[1]
    nt = t // tm
    const = lambda a: pl.BlockSpec(a.shape, lambda b, i: (0,) * a.ndim)
    return pl.pallas_call(
        _conv_prompt_kernel,
        grid=(nb, nt),
        in_specs=[pl.BlockSpec((tm, d), lambda b, i: (b * nt + i, 0)),
                  const(g_mix), const(wcu), const(wdw), const(bdw), const(lng), const(lnb),
                  const(wco), const(bco)],
        out_specs=(pl.BlockSpec((tm, d), lambda b, i: (b * nt + i, 0)),
                   pl.BlockSpec((None, CONV_WIDTH - 1, dc), lambda b, i: (b, 0, 0))),
        out_shape=(jax.ShapeDtypeStruct((n, d), F32),
                   jax.ShapeDtypeStruct((nb, CONV_WIDTH - 1, dc), F32)),
        scratch_shapes=[pltpu.VMEM((HIST + tm, dc), F32)],
        compiler_params=_cparams(("arbitrary", "arbitrary")),
        name="conv_prompt",
    )(x2d, g_mix, wcu, wdw, bdw, lng, lnb, wco, bco)


def _conv_sample_kernel(x_ref, st_ref, g_ref, wcu_ref, wdw_ref, bdw_ref, lng_ref, lnb_ref, wco_ref, bco_ref,
                        y_ref, nst_ref, pad_ref, glu_ref, dw_ref, *, ds):
    dc = wdw_ref.shape[1]
    nseq = st_ref.shape[0]
    w1 = CONV_WIDTH - 1
    xn = _rms(x_ref[...], g_ref[...]).astype(BF16)
    cu = _dot(xn, wcu_ref[...])
    glu_ref[...] = cu[:, 0:dc] * jax.nn.sigmoid(cu[:, dc:2 * dc])

    def body(b, carry):
        r0 = pl.multiple_of(b * ds, ds)
        pad_ref[HIST - w1:HIST, :] = st_ref[b]
        pad_ref[HIST:HIST + ds, :] = glu_ref[pl.ds(r0, ds), :]
        dw_ref[pl.ds(r0, ds), :] = _depthwise(pad_ref, wdw_ref, ds)
        nst_ref[b] = pad_ref[HIST + ds - w1:HIST + ds, :]
        return carry

    lax.fori_loop(0, nseq, body, 0)
    y_ref[...] = _conv_tail(dw_ref[...], bdw_ref[...], lng_ref[...], lnb_ref[...], wco_ref[...], bco_ref[...])


def _conv_sample(x2d, state, g_mix, wcu, wdw, bdw, lng, lnb, wco, bco, ds):
    n, d = x2d.shape
    dc = wdw.shape[1]
    full = lambda a: pl.BlockSpec(a.shape, lambda i: (0,) * a.ndim)
    args = (x2d, state, g_mix, wcu, wdw, bdw, lng, lnb, wco, bco)
    return pl.pallas_call(
        functools.partial(_conv_sample_kernel, ds=ds),
        grid=(1,),
        in_specs=[full(a) for a in args],
        out_specs=(pl.BlockSpec((n, d), lambda i: (0, 0)),
                   pl.BlockSpec(state.shape, lambda i: (0, 0, 0))),
        out_shape=(jax.ShapeDtypeStruct((n, d), F32),
                   jax.ShapeDtypeStruct(state.shape, F32)),
        scratch_shapes=[pltpu.VMEM((HIST + ds, dc), F32), pltpu.VMEM((n, dc), F32),
                        pltpu.VMEM((n, dc), F32)],
        compiler_params=_cparams(("arbitrary",)),
        name="conv_sample",
    )(*args)


def _kth_largest_key(count_ge, rows, keep):
    def bit_body(it, th_u):
        cand_u = th_u | lax.shift_left(jnp.int32(1), 31 - it)
        cnt = count_ge(cand_u ^ jnp.int32(INT_MIN))
        return jnp.where(cnt >= keep, cand_u, th_u)

    th_u = lax.fori_loop(0, 32, bit_body, jnp.zeros((rows, 1), I32))
    return th_u ^ jnp.int32(INT_MIN)


def _prompt_attn_kernel(qi_ref, wi_ref, ki_ref, q_ref, k_ref, v_ref, o_ref, keys_ref, *, ch, keep):
    qb = Q_BLOCK
    i = pl.program_id(1)
    t0 = i * qb
    n_ch = (t0 + qb + ch - 1) // ch

    qi = qi_ref[...]
    wi = wi_ref[...]
    qih = [qi[:, h * IDX_DIM:(h + 1) * IDX_DIM] for h in range(N_IDX_HEADS)]
    wih = [wi[:, h:h + 1] for h in range(N_IDX_HEADS)]
    row_t = t0 + lax.broadcasted_iota(I32, (qb, ch), 0)
    col = lax.broadcasted_iota(I32, (qb, ch), 1)

    def score_body(c, carry):
        s0 = pl.multiple_of(c * ch, ch)
        kic = ki_ref[pl.ds(s0, ch), :]
        acc = None
        for h in range(N_IDX_HEADS):
            term = jnp.maximum(_dot_nt(qih[h], kic), 0.0) * wih[h]
            acc = term if acc is None else acc + term
        key = _to_key(acc * IDX_HEAD_SCALE)
        keys_ref[c] = jnp.where(s0 + col <= row_t, key, jnp.int32(INT_MIN))
        return carry

    lax.fori_loop(0, n_ch, score_body, 0)

    def count_ge(cand):
        def body(c, acc):
            return acc + _fold_lanes((keys_ref[c] >= cand).astype(I32))
        acc = lax.fori_loop(0, n_ch, body, jnp.zeros((qb, LANES), I32))
        return jnp.sum(acc, axis=-1, keepdims=True)

    th = _kth_largest_key(count_ge, qb, keep)
    cnt_ge = count_ge(th)
    need_fix = (cnt_ge > keep) & (th > jnp.int32(INT_MIN))

    @pl.when(jnp.max(need_fix.astype(I32)) > 0)
    def _():
        need = (keep - count_ge(th + 1)).astype(F32)
        tri = (lax.broadcasted_iota(I32, (ch, ch), 0) <= lax.broadcasted_iota(I32, (ch, ch), 1)).astype(BF16)

        def body(c, run):
            kc = keys_ref[c]
            eq = (kc == th) & need_fix
            eqf = jnp.where(eq, 1.0, 0.0)
            pos = run + _dot(eqf.astype(BF16), tri)
            keys_ref[c] = jnp.where(eq & (pos > need), th - 1, kc)
            return run + jnp.sum(eqf, axis=-1, keepdims=True)

        lax.fori_loop(0, n_ch, body, jnp.zeros((qb, 1), F32))

    th_eff = jnp.maximum(th, jnp.int32(INT_MIN + 1))

    q = q_ref[...]
    lo = lax.broadcasted_iota(I32, (qb, LANES), 1) < HEAD_DIM
    qm = []
    for h in range(N_HEADS):
        qp = q[:, (h // 2) * LANES:(h // 2 + 1) * LANES]
        qm.append(jnp.where(lo == (h % 2 == 0), qp, jnp.zeros_like(qp)))

    def attn_body(c, carry):
        ms, ls, accs = carry
        s0 = pl.multiple_of(c * ch, ch)
        sel = keys_ref[c] >= th_eff
        nms, nls, naccs = [], [], []
        for p in range(N_HEADS // 2):
            kp = k_ref[pl.ds(s0, ch), p * LANES:(p + 1) * LANES]
            vp = v_ref[pl.ds(s0, ch), p * LANES:(p + 1) * LANES]
            pv, al = [], []
            for u in range(2):
                h = 2 * p + u
                lg = jnp.where(sel, _dot_nt(qm[h], kp), NEG_BIG)
                m_new = jnp.maximum(ms[h], jnp.max(lg, axis=-1, keepdims=True))
                alpha = jnp.exp(ms[h] - m_new)
                pe = jnp.exp(lg - m_new)
                nls.append(ls[h] * alpha + jnp.sum(pe, axis=-1, keepdims=True))
                nms.append(m_new)
                pv.append(_dot(pe.astype(BF16), vp))
                al.append(alpha)
            naccs.append(accs[p] * jnp.where(lo, al[0], al[1]) + jnp.where(lo, pv[0], pv[1]))
        return tuple(nms), tuple(nls), tuple(naccs)

    init = (tuple(jnp.full((qb, 1), NEG_BIG, F32) for _ in range(N_HEADS)),
            tuple(jnp.zeros((qb, 1), F32) for _ in range(N_HEADS)),
            tuple(jnp.zeros((qb, LANES), F32) for _ in range(N_HEADS // 2)))
    ms, ls, accs = lax.fori_loop(0, n_ch, attn_body, init)
    outs = [accs[p] / jnp.where(lo, ls[2 * p], ls[2 * p + 1]) for p in range(N_HEADS // 2)]
    o_ref[...] = jnp.concatenate(outs, axis=1).astype(BF16)


def _prompt_attention(qi, wi, ki, q, k, v, nb, ch):
    n = q.shape[0]
    t = n // nb
    nq = t // Q_BLOCK
    keep = min(TOPK_MAX, t // 4)
    aw = ATTN_WIDTH
    qblk = lambda w: pl.BlockSpec((Q_BLOCK, w), lambda b, i: (b * nq + i, 0))
    batch = lambda w: pl.BlockSpec((t, w), lambda b, i: (b, 0), pipeline_mode=pl.Buffered(1))
    return pl.pallas_call(
        functools.partial(_prompt_attn_kernel, ch=ch, keep=keep),
        grid=(nb, nq),
        in_specs=[qblk(aw), qblk(N_IDX_HEADS), batch(IDX_DIM), qblk(aw), batch(aw), batch(aw)],
        out_specs=qblk(aw),
        out_shape=jax.ShapeDtypeStruct((n, aw), BF16),
        scratch_shapes=[pltpu.VMEM((t // ch, Q_BLOCK, ch), I32)],
        compiler_params=_cparams(("arbitrary", "arbitrary")),
        name="prompt_attention",
    )(qi, wi, ki, q, k, v)


def _sample_score_kernel(pt_ref, a_ref, w_ref, *refs, pg, ds):
    pages = refs[:pg]
    kin_ref, keys_ref, keysn_ref = refs[pg:pg + 3]
    g = pl.program_id(1)
    a = a_ref[...]
    w = w_ref[...]

    def score_page(kpage):
        r = jnp.maximum(_dot_nt(a, kpage), 0.0) * w
        s = r[0:ds]
        for h in range(1, N_IDX_HEADS):
            s = s + r[h * ds:(h + 1) * ds]
        return s * IDX_HEAD_SCALE

    for j in range(pg):
        keys_ref[:, j * PAGE_SIZE:(j + 1) * PAGE_SIZE] = _to_key(score_page(pages[j][...].astype(BF16)))

    @pl.when(g == pl.num_programs(1) - 1)
    def _():
        sn = _to_key(score_page(kin_ref[...]))
        row = lax.broadcasted_iota(I32, sn.shape, 0)
        colv = lax.broadcasted_iota(I32, sn.shape, 1)
        keysn_ref[...] = jnp.where(colv <= row, sn, jnp.int32(INT_MIN))


def _sample_scores(page_table, a_mat, w_col, cache_kidx, ki_new, pg, ds):
    db, npages = page_table.shape
    ng = npages // pg
    page_spec = lambda j: pl.BlockSpec((None, PAGE_SIZE, IDX_DIM),
                                       lambda b, g, pt: (pt[b, g * pg + j], 0, 0))
    grid_spec = pltpu.PrefetchScalarGridSpec(
        num_scalar_prefetch=1,
        grid=(db, ng),
        in_specs=[pl.BlockSpec((None,) + a_mat.shape[1:], lambda b, g, pt: (b, 0, 0)),
                  pl.BlockSpec((None,) + w_col.shape[1:], lambda b, g, pt: (b, 0, 0))]
                 + [page_spec(j) for j in range(pg)]
                 + [pl.BlockSpec((None, PAGE_SIZE, IDX_DIM), lambda b, g, pt: (b, 0, 0))],
        out_specs=(pl.BlockSpec((None, ds, pg * PAGE_SIZE), lambda b, g, pt: (b, 0, g)),
                   pl.BlockSpec((None, ds, PAGE_SIZE), lambda b, g, pt: (b, 0, 0))),
    )
    return pl.pallas_call(
        functools.partial(_sample_score_kernel, pg=pg, ds=ds),
        grid_spec=grid_spec,
        out_shape=(jax.ShapeDtypeStruct((db, ds, npages * PAGE_SIZE), I32),
                   jax.ShapeDtypeStruct((db, ds, PAGE_SIZE), I32)),
        compiler_params=_cparams(("arbitrary", "arbitrary")),
        name="sample_scores",
    )(page_table, a_mat, w_col, *([cache_kidx] * pg), ki_new)


def _sample_thresh_kernel(keys_ref, keysn_ref, th_ref, okeys_ref, okeysn_ref, *, keep):
    ds, past = keys_ref.shape
    nblk = past // LANES
    okeys_ref[...] = keys_ref[...]
    okeysn_ref[...] = keysn_ref[...]

    def count_ge(cand):
        acc = _fold_lanes((okeys_ref[...] >= cand).astype(I32)) + (okeysn_ref[...] >= cand).astype(I32)
        return jnp.sum(acc, axis=-1, keepdims=True)

    th = _kth_largest_key(count_ge, ds, keep)
    cnt_ge = count_ge(th)
    need_fix = (cnt_ge > keep) & (th > jnp.int32(INT_MIN))

    @pl.when(jnp.max(need_fix.astype(I32)) > 0)
    def _():
        need = (keep - count_ge(th + 1)).astype(F32)
        tri = (lax.broadcasted_iota(I32, (LANES, LANES), 0)
               <= lax.broadcasted_iota(I32, (LANES, LANES), 1)).astype(BF16)

        def fix(kc, run):
            eq = (kc == th) & need_fix
            eqf = jnp.where(eq, 1.0, 0.0)
            pos = run + _dot(eqf.astype(BF16), tri)
            return jnp.where(eq & (pos > need), th - 1, kc), run + jnp.sum(eqf, axis=-1, keepdims=True)

        def body(j, run):
            s0 = pl.multiple_of(j * LANES, LANES)
            kc, run = fix(okeys_ref[:, pl.ds(s0, LANES)], run)
            okeys_ref[:, pl.ds(s0, LANES)] = kc
            return run

        run = lax.fori_loop(0, nblk, body, jnp.zeros((ds, 1), F32))
        kc, _ = fix(okeysn_ref[...], run)
        okeysn_ref[...] = kc

    th_ref[...] = jnp.broadcast_to(jnp.maximum(th, jnp.int32(INT_MIN + 1)), th_ref.shape)


def _sample_threshold(keys, keysn, keep):
    db, ds, past = keys.shape
    blk = lambda w: pl.BlockSpec((None, ds, w), lambda b: (b, 0, 0))
    return pl.pallas_call(
        functools.partial(_sample_thresh_kernel, keep=keep),
        grid=(db,),
        in_specs=[blk(past), blk(PAGE_SIZE)],
        out_specs=(blk(LANES), blk(past), blk(PAGE_SIZE)),
        out_shape=(jax.ShapeDtypeStruct((db, ds, LANES), I32),
                   jax.ShapeDtypeStruct((db, ds, past), I32),
                   jax.ShapeDtypeStruct((db, ds, PAGE_SIZE), I32)),
        compiler_params=_cparams(("arbitrary",)),
        name="sample_threshold",
    )(keys, keysn)


def _sample_attn_kernel(pt_ref, qbd_ref, th_ref, keys_ref, keysn_ref, kn_ref, vn_ref, *refs, pg, ds):
    kpages = refs[:pg]
    vpages = refs[pg:2 * pg]
    o_ref, m_ref, l_ref, acc_ref = refs[2 * pg:2 * pg + 4]
    g = pl.program_id(1)
    rows = N_HEADS * ds

    @pl.when(g == 0)
    def _():
        m_ref[...] = jnp.full(m_ref.shape, NEG_BIG, F32)
        l_ref[...] = jnp.zeros(l_ref.shape, F32)
        acc_ref[...] = jnp.zeros(acc_ref.shape, F32)

    qbd = qbd_ref[...]
    th = th_ref[...]

    def attend(kpage, vpage, keys):
        sel = keys >= th
        lg = _dot_nt(qbd, kpage).reshape(N_HEADS, ds, PAGE_SIZE)
        lg = jnp.where(sel[None], lg, NEG_BIG).reshape(rows, PAGE_SIZE)
        m_old = m_ref[...]
        m_new = jnp.maximum(m_old, jnp.max(lg, axis=-1, keepdims=True))
        alpha = jnp.exp(m_old - m_new)
        pe = jnp.exp(lg - m_new)
        l_ref[...] = l_ref[...] * alpha + jnp.sum(pe, axis=-1, keepdims=True)
        acc_ref[...] = acc_ref[...] * alpha + _dot(pe.astype(BF16), vpage)
        m_ref[...] = m_new

    for j in range(pg):
        attend(kpages[j][...].astype(BF16), vpages[j][...].astype(BF16),
               keys_ref[:, j * PAGE_SIZE:(j + 1) * PAGE_SIZE])

    @pl.when(g == pl.num_programs(1) - 1)
    def _():
        attend(kn_ref[...], vn_ref[...], keysn_ref[...])
        res = acc_ref[...] / l_ref[...]
        head = lax.broadcasted_iota(I32, (ds, ATTN_WIDTH), 1) // HEAD_DIM
        out = jnp.zeros((ds, ATTN_WIDTH), F32)
        for h in range(N_HEADS):
            out = out + jnp.where(head == h, res[h * ds:(h + 1) * ds, :], 0.0)
        o_ref[...] = out.astype(BF16)


def _sample_attention(page_table, qbd, th, keys, keysn, k_new, v_new, cache_k, cache_v, pg, ds):
    db, npages = page_table.shape
    ng = npages // pg
    aw = ATTN_WIDTH
    rows = N_HEADS * ds
    per_b = lambda shp: pl.BlockSpec((None,) + shp, lambda b, g, pt: (b, 0, 0))
    page_spec = lambda j: pl.BlockSpec((None, PAGE_SIZE, aw), lambda b, g, pt: (pt[b, g * pg + j], 0, 0))
    grid_spec = pltpu.PrefetchScalarGridSpec(
        num_scalar_prefetch=1,
        grid=(db, ng),
        in_specs=[per_b((rows, aw)), per_b((ds, LANES)),
                  pl.BlockSpec((None, ds, pg * PAGE_SIZE), lambda b, g, pt: (b, 0, g)),
                  per_b((ds, PAGE_SIZE)), per_b((PAGE_SIZE, aw)), per_b((PAGE_SIZE, aw))]
                 + [page_spec(j) for j in range(pg)] + [page_spec(j) for j in range(pg)],
        out_specs=per_b((ds, aw)),
        scratch_shapes=[pltpu.VMEM((rows, 1), F32), pltpu.VMEM((rows, 1), F32),
                        pltpu.VMEM((rows, aw), F32)],
    )
    return pl.pallas_call(
        functools.partial(_sample_attn_kernel, pg=pg, ds=ds),
        grid_spec=grid_spec,
        out_shape=jax.ShapeDtypeStruct((db, ds, aw), BF16),
        compiler_params=_cparams(("arbitrary", "arbitrary")),
        name="sample_attention",
    )(page_table, qbd, th, keys, keysn, k_new, v_new, *([cache_k] * pg), *([cache_v] * pg))


def _merge_kernel(x_ref, o_ref, yc_ref, gmix_ref, wgt_ref, wao_ref, wmo_ref, gffn_ref, wr_ref, br_ref,
                  x1_ref, xn2_ref, ti_ref, tw_ref):
    x = x_ref[...]
    d = x.shape[1]
    xn = _rms(x, gmix_ref[...]).astype(BF16)
    gate = jax.nn.sigmoid(_dot(xn, wgt_ref[...]))
    ya = _dot(o_ref[...], wao_ref[...])
    mix = gate[:, 0:d] * ya + gate[:, d:2 * d] * yc_ref[...]
    x1 = x + _dot(mix.astype(BF16), wmo_ref[...])
    x1_ref[...] = x1
    xn2 = _rms(x1, gffn_ref[...]).astype(BF16)
    xn2_ref[...] = xn2
    logits = _dot(xn2, wr_ref[...]) + br_ref[...]
    lane = lax.broadcasted_iota(I32, logits.shape, 1)
    vals, idxs = [], []
    for _ in range(TOP_K):
        m = jnp.max(logits, axis=-1, keepdims=True)
        idx = jnp.min(jnp.where(logits == m, lane, LANES), axis=-1, keepdims=True)
        vals.append(m)
        idxs.append(idx)
        logits = jnp.where(lane == idx, -jnp.inf, logits)
    es = [jnp.exp(v - vals[0]) for v in vals]
    den = es[0]
    for e in es[1:]:
        den = den + e
    tw = jnp.zeros(logits.shape, F32)
    ti = jnp.zeros(logits.shape, I32)
    for j in range(TOP_K):
        tw = jnp.where(lane == j, es[j] / den, tw)
        ti = jnp.where(lane == j, idxs[j], ti)
    tw_ref[...] = tw
    ti_ref[...] = ti


def _merge(x2d, o, yc, g_mix, wgt, wao, wmo, g_ffn, wr, br, tm):
    n, d = x2d.shape
    row = lambda w: pl.BlockSpec((tm, w), lambda i: (i, 0))
    const = lambda a: pl.BlockSpec(a.shape, lambda i: (0, 0))
    return pl.pallas_call(
        _merge_kernel,
        grid=(n // tm,),
        in_specs=[row(d), row(ATTN_WIDTH), row(d), const(g_mix), const(wgt), const(wao), const(wmo),
                  const(g_ffn), const(wr), const(br)],
        out_specs=(row(d), row(d), row(LANES), row(LANES)),
        out_shape=(jax.ShapeDtypeStruct((n, d), F32), jax.ShapeDtypeStruct((n, d), BF16),
                   jax.ShapeDtypeStruct((n, LANES), I32), jax.ShapeDtypeStruct((n, LANES), F32)),
        compiler_params=_cparams(("arbitrary",)),
        name="merge_router",
    )(x2d, o, yc, g_mix, wgt, wao, wmo, g_ffn, wr, br)


def _moe_kernel(te_ref, nu_ref, xs_ref, wg_ref, bg_ref, wu_ref, bu_ref, wd_ref, bd_ref, y_ref,
                wgb_ref, wub_ref, wdb_ref):
    i = pl.program_id(0)
    active = i < nu_ref[0]
    new_expert = (i == 0) | (te_ref[i] != te_ref[jnp.maximum(i - 1, 0)])

    @pl.when(active & new_expert)
    def _():
        wgb_ref[...] = wg_ref[...].astype(BF16)
        wub_ref[...] = wu_ref[...].astype(BF16)
        wdb_ref[...] = wd_ref[...].astype(BF16)

    @pl.when(active)
    def _():
        x = xs_ref[...]
        gate = jnp.minimum(_dot(x, wgb_ref[...]) + bg_ref[...], SWIGLU_LIMIT)
        up = jnp.clip(_dot(x, wub_ref[...]) + bu_ref[...], -SWIGLU_LIMIT, SWIGLU_LIMIT)
        act = (up + 1.0) * gate * jax.nn.sigmoid(SWIGLU_ALPHA * gate)
        y_ref[...] = _dot(act.astype(BF16), wdb_ref[...]) + bd_ref[...]

    @pl.when(jnp.logical_not(active))
    def _():
        y_ref[...] = jnp.zeros(y_ref.shape, F32)


def _moe_experts(tile_expert, n_used, xs, w_gate, b_gate, w_up, b_up, w_down, b_down, tme):
    r, d = xs.shape
    ne, _, f = w_gate.shape
    wspec = lambda k, n: pl.BlockSpec((None, k, n), lambda i, te, nu: (te[i], 0, 0))
    bspec = lambda n: pl.BlockSpec((None, 1, n), lambda i, te, nu: (te[i], 0, 0))
    grid_spec = pltpu.PrefetchScalarGridSpec(
        num_scalar_prefetch=2,
        grid=(r // tme,),
        in_specs=[pl.BlockSpec((tme, d), lambda i, te, nu: (i, 0)),
                  wspec(d, f), bspec(f), wspec(d, f), bspec(f), wspec(f, d), bspec(d)],
        out_specs=pl.BlockSpec((tme, d), lambda i, te, nu: (i, 0)),
        scratch_shapes=[pltpu.VMEM((d, f), BF16), pltpu.VMEM((d, f), BF16), pltpu.VMEM((f, d), BF16)],
    )
    return pl.pallas_call(
        _moe_kernel,
        grid_spec=grid_spec,
        out_shape=jax.ShapeDtypeStruct((r, d), F32),
        compiler_params=_cparams(("arbitrary",)),
        name="moe_experts",
    )(tile_expert, n_used, xs, w_gate, b_gate.reshape(ne, 1, f), w_up, b_up.reshape(ne, 1, f),
      w_down, b_down.reshape(ne, 1, d))


def _final_kernel(x1_ref, yg_ref, tw_ref, g_ref, y_ref):
    tw = tw_ref[...]
    moe = tw[:, 0:1] * yg_ref[0]
    for j in range(1, TOP_K):
        moe = moe + tw[:, j:j + 1] * yg_ref[j]
    y_ref[...] = _rms(x1_ref[...] + moe, g_ref[...])


def _final(x1, yg, tw, g_final, row0, tm):
    n, d = x1.shape
    assert row0 % tm == 0
    off = row0 // tm
    return pl.pallas_call(
        _final_kernel,
        grid=(n // tm,),
        in_specs=[pl.BlockSpec((tm, d), lambda i: (i, 0)),
                  pl.BlockSpec((TOP_K, tm, d), lambda i: (0, off + i, 0)),
                  pl.BlockSpec((tm, LANES), lambda i: (off + i, 0)),
                  pl.BlockSpec(g_final.shape, lambda i: (0, 0))],
        out_specs=pl.BlockSpec((tm, d), lambda i: (i, 0)),
        out_shape=jax.ShapeDtypeStruct((n, d), F32),
        compiler_params=_cparams(("arbitrary",)),
        name="combine_final_norm",
    )(x1, yg, tw, g_final)


def _rope_tables(pos):
    half = ROT_DIM // 2
    inv_freq = jnp.exp(-math.log(ROPE_THETA) * jnp.arange(half, dtype=F32) * (2.0 / ROT_DIM))
    ang = pos.astype(F32)[:, None] * inv_freq[None, :]
    cos, sin = jnp.cos(ang), jnp.sin(ang)
    n = pos.shape[0]
    rest = HEAD_DIM - ROT_DIM
    z8 = jnp.zeros((n, half), F32)
    zr = jnp.zeros((n, rest), F32)
    c = jnp.concatenate([cos, cos, jnp.ones((n, rest), F32)], axis=1)
    s1 = jnp.concatenate([-sin, z8, zr], axis=1)
    s2 = jnp.concatenate([z8, sin, zr], axis=1)
    rep = LANES // HEAD_DIM
    return tuple(jnp.tile(a, (1, rep)) for a in (c, s1, s2))


def _moe_plan(top_i, n_experts, tme):
    p = top_i.size
    flat_e = top_i.reshape(-1)
    onehot = (flat_e[:, None] == jnp.arange(n_experts, dtype=I32)[None, :]).astype(I32)
    csum = jnp.cumsum(onehot, axis=0)
    rank = jnp.sum((csum - onehot) * onehot, axis=1)
    counts = csum[-1]
    padded = ((counts + tme - 1) // tme) * tme
    ends = jnp.cumsum(padded)
    offs = ends - padded
    dest = offs[flat_e] + rank
    n_rows = ((p + n_experts * (tme - 1)) // tme) * tme
    src_tok = jnp.zeros((n_rows,), I32).at[dest].set(jnp.arange(p, dtype=I32) // TOP_K)
    n_tiles = n_rows // tme
    tile_start = jnp.arange(n_tiles, dtype=I32) * tme
    tile_expert = jnp.minimum(jnp.searchsorted(ends, tile_start, side="right"), n_experts - 1).astype(I32)
    n_used = (ends[-1] // tme).astype(I32).reshape(1)
    last_e = tile_expert[jnp.maximum(n_used[0] - 1, 0)]
    tile_expert = jnp.where(jnp.arange(n_tiles) < n_used[0], tile_expert, last_e)
    return src_tok, dest, tile_expert, n_used


def kernel(x_prompt, x_sample, cache_k, cache_v, cache_kidx, state_conv, page_table, g_mix, w_in, w_attn_out,
           w_dw, b_dw, ln_g, ln_b, w_conv_out, b_conv_out, w_mix_out, g_ffn, w_router, b_router, w_gate,
           b_gate, w_up, b_up, w_down, b_down, g_final, *, attn_chunk=512, tile_rows=256, conv_rows=512,
           moe_rows=256, page_group=8):
    nb, t, d = x_prompt.shape
    db, ds, _ = x_sample.shape
    depth = g_mix.shape[0]
    assert depth == 1 and ds == SUBLANES
    n_pool = cache_k.shape[1]
    npages = page_table.shape[1]
    past = npages * PAGE_SIZE
    aw = ATTN_WIDTH
    dc = d // 2
    n_experts = w_router.shape[-1]
    np_, ns = nb * t, db * ds

    w_in0 = w_in[0]
    c_small = 4 * aw
    c_cu = c_small + IDX_DIM + N_IDX_HEADS
    c_gt = c_cu + 2 * dc
    w_main = w_in0[:, :c_small].astype(BF16)
    w_small = jnp.pad(w_in0[:, c_small:c_cu], ((0, 0), (0, LANES - (c_cu - c_small)))).astype(BF16)
    w_cu = w_in0[:, c_cu:c_gt].astype(BF16)
    w_gt = w_in0[:, c_gt:].astype(BF16)
    w_ao = w_attn_out[0].astype(BF16)
    w_co = w_conv_out[0].astype(BF16)
    w_mo = w_mix_out[0].astype(BF16)
    w_r = jnp.pad(w_router[0], ((0, 0), (0, LANES - n_experts))).astype(BF16)
    b_r = jnp.pad(b_router[0], (0, LANES - n_experts), constant_values=NEG_BIG).reshape(1, LANES)
    gm = g_mix[0].reshape(1, d)
    gf = g_ffn[0].reshape(1, d)
    row = lambda a: a.reshape(1, -1)
    conv_w = (w_dw[0], row(b_dw[0]), row(ln_g[0]), row(ln_b[0]), w_co, row(b_conv_out[0]))

    xp = x_prompt.reshape(np_, d)
    xs = x_sample.reshape(ns, d)
    pos_p = jnp.tile(jnp.arange(t, dtype=I32), nb)
    pos_s = jnp.tile(past + jnp.arange(ds, dtype=I32), db)

    q_p, kf_p, kb_p, vf_p, vb_p, qi_p, kif_p, kib_p, wi_p = _in_projection(
        xp, gm, w_main, w_small, _rope_tables(pos_p), tile_rows)
    o_p = _prompt_attention(qi_p, wi_p, kib_p, q_p, kb_p, vb_p, nb, min(attn_chunk, t))
    yc_p, conv_p = _conv_prompt(xp, nb, gm, w_cu, *conv_w, min(conv_rows, t))
    x1_p, xn2_p, ti_p, tw_p = _merge(xp, o_p, yc_p, gm, w_gt, w_ao, w_mo, gf, w_r, b_r, tile_rows)

    q_s, kf_s, kb_s, vf_s, vb_s, qi_s, kif_s, kib_s, wi_s = _in_projection(
        xs, gm, w_main, w_small, _rope_tables(pos_s), ns)
    a_mat = qi_s.reshape(db, ds, N_IDX_HEADS, IDX_DIM).transpose(0, 2, 1, 3).reshape(db, N_IDX_HEADS * ds, IDX_DIM)
    w_col = wi_s.reshape(db, ds, N_IDX_HEADS).transpose(0, 2, 1).reshape(db, N_IDX_HEADS * ds, 1)
    pad_page = lambda a: jnp.pad(a.reshape(db, ds, -1), ((0, 0), (0, PAGE_SIZE - ds), (0, 0)))
    keys, keysn = _sample_scores(page_table, a_mat, w_col, cache_kidx[0], pad_page(kib_s), page_group, ds)
    keep_s = min(TOPK_MAX, (past + ds) // 4)
    th, keys, keysn = _sample_threshold(keys, keysn, keep_s)
    eye = jnp.eye(N_HEADS, dtype=BF16)
    qbd = (q_s.reshape(db, ds, N_HEADS, 1, HEAD_DIM).transpose(0, 2, 1, 3, 4)
           * eye[None, :, None, :, None]).reshape(db, N_HEADS * ds, aw)
    o_s = _sample_attention(page_table, qbd, th, keys, keysn, pad_page(kb_s), pad_page(vb_s),
                            cache_k[0].reshape(n_pool, PAGE_SIZE, aw), cache_v[0].reshape(n_pool, PAGE_SIZE, aw),
                            page_group, ds).reshape(ns, aw)
    yc_s, conv_s = _conv_sample(xs, state_conv[0], gm, w_cu, *conv_w, ds)
    x1_s, xn2_s, ti_s, tw_s = _merge(xs, o_s, yc_s, gm, w_gt, w_ao, w_mo, gf, w_r, b_r, ns)

    xn2 = jnp.concatenate([xn2_p, xn2_s], axis=0)
    top_i = jnp.concatenate([ti_p, ti_s], axis=0)[:, :TOP_K]
    top_w = jnp.concatenate([tw_p, tw_s], axis=0)
    src_tok, dest, tile_expert, n_used = _moe_plan(top_i, n_experts, moe_rows)
    x_sorted = jnp.take(xn2, src_tok, axis=0)
    y_sorted = _moe_experts(tile_expert, n_used, x_sorted, w_gate[0], b_gate[0], w_up[0], b_up[0],
                            w_down[0], b_down[0], moe_rows)
    yg = jnp.take(y_sorted, dest.reshape(-1, TOP_K).T, axis=0)
    gfin = g_final.reshape(1, d)
    y_p = _final(x1_p, yg, top_w, gfin, 0, tile_rows)
    y_s = _final(x1_s, yg, top_w, gfin, np_, ns)

    return (y_p.reshape(nb, t, d), y_s.reshape(db, ds, d),
            kf_p.reshape(1, nb, t, N_HEADS, HEAD_DIM), vf_p.reshape(1, nb, t, N_HEADS, HEAD_DIM),
            kif_p.reshape(1, nb, t, IDX_DIM), conv_p[None],
            kf_s.reshape(1, db, ds, N_HEADS, HEAD_DIM), vf_s.reshape(1, db, ds, N_HEADS, HEAD_DIM),
            kif_s.reshape(1, db, ds, IDX_DIM), conv_s[None])
```

```python
import functools
import math

import jax
import jax.numpy as jnp
from jax import lax
from jax.experimental import pallas as pl
from jax.experimental.pallas import tpu as pltpu

N_HEADS = 8
HEAD_DIM = 64
ATTN_WIDTH = N_HEADS * HEAD_DIM
N_IDX_HEADS = 8
IDX_DIM = 64
TOPK_MAX = 256
ROPE_THETA = 500000.0
ROT_DIM = HEAD_DIM // 4
CONV_WIDTH = 31
N_BRANCHES = 2
TOP_K = 4
SWIGLU_LIMIT = 7.0
SWIGLU_ALPHA = 1.702
NORM_EPS = 1e-5
PAGE_SIZE = 128
Q_BLOCK = 128

LANES = 128
SUBLANES = 8
VMEM_LIMIT = 56 * 1024 * 1024
INT_MIN = -(2 ** 31)
KEY_NEG_INF = 0x807FFFFF - 2 ** 32
F32_MAX = 3.4028234663852886e38
NEG_BIG = -1e30
QK_SCALE = HEAD_DIM ** -0.5
IDX_QK_SCALE = IDX_DIM ** -0.5
IDX_HEAD_SCALE = N_IDX_HEADS ** -0.5
BITS_PER_CHECK = 4

F32 = jnp.float32
BF16 = jnp.bfloat16
I32 = jnp.int32


def _cparams(sem):
    return pltpu.CompilerParams(dimension_semantics=sem, vmem_limit_bytes=VMEM_LIMIT)


def _dot(a, b):
    return jnp.dot(a, b, preferred_element_type=F32)


def _dot_nt(a, b):
    return lax.dot_general(a, b, (((1,), (1,)), ((), ())), preferred_element_type=F32)


def _rms(x, g):
    return x * lax.rsqrt(jnp.mean(x * x, axis=-1, keepdims=True) + NORM_EPS) * g


def _rope(seg, c, s1, s2):
    outs = []
    for j in range(seg.shape[1] // LANES):
        blk = seg[:, j * LANES:(j + 1) * LANES]
        outs.append(blk * c + pltpu.roll(blk, LANES - ROT_DIM // 2, 1) * s1
                    + pltpu.roll(blk, ROT_DIM // 2, 1) * s2)
    return outs[0] if len(outs) == 1 else jnp.concatenate(outs, axis=1)


def _key_to_float(key):
    key = jnp.maximum(key, jnp.int32(KEY_NEG_INF))
    bits = jnp.where(key < 0, key ^ jnp.int32(0x7FFFFFFF), key)
    return lax.bitcast_convert_type(bits, F32)


def _fold_lanes(m):
    parts = [m[:, j * LANES:(j + 1) * LANES] for j in range(m.shape[1] // LANES)]
    while len(parts) > 1:
        nxt = [parts[j] + parts[j + 1] for j in range(0, len(parts) - 1, 2)]
        if len(parts) % 2:
            nxt.append(parts[-1])
        parts = nxt
    return parts[0]


def _inproj_kernel(x_ref, g_ref, wm_ref, ws_ref, c_ref, s1_ref, s2_ref,
                   q_ref, kf_ref, kb_ref, vf_ref, vt_ref, qi_ref, kif_ref, kib_ref, wi_ref):
    xn = _rms(x_ref[...], g_ref[...]).astype(BF16)
    c, s1, s2 = c_ref[...], s1_ref[...], s2_ref[...]
    aw = ATTN_WIDTH
    h = _dot(xn, wm_ref[...])
    q_ref[...] = (_rope(h[:, 0:aw], c, s1, s2) * QK_SCALE).astype(BF16)
    k = _rope(h[:, aw:2 * aw], c, s1, s2)
    kf_ref[...] = k
    kb_ref[...] = k.astype(BF16)
    v = h[:, 2 * aw:3 * aw]
    vf_ref[...] = v
    vt_ref[...] = v.T.astype(BF16)
    qi_ref[...] = (_rope(h[:, 3 * aw:4 * aw], c, s1, s2) * IDX_QK_SCALE).astype(BF16)
    hs = _dot(xn, ws_ref[...])
    ki = _rope(hs, c, s1, s2)[:, 0:IDX_DIM]
    kif_ref[...] = ki
    kib_ref[...] = ki.astype(BF16)
    wi_ref[...] = hs[:, IDX_DIM:IDX_DIM + N_IDX_HEADS]


def _in_projection(x2d, g_mix, w_main, w_small, tabs, tm):
    n, d = x2d.shape
    aw = ATTN_WIDTH
    row = lambda w: pl.BlockSpec((tm, w), lambda i: (i, 0))
    const = lambda a: pl.BlockSpec(a.shape, lambda i: (0, 0))
    out_shapes = (
        jax.ShapeDtypeStruct((n, aw), BF16),
        jax.ShapeDtypeStruct((n, aw), F32),
        jax.ShapeDtypeStruct((n, aw), BF16),
        jax.ShapeDtypeStruct((n, aw), F32),
        jax.ShapeDtypeStruct((aw, n), BF16),
        jax.ShapeDtypeStruct((n, aw), BF16),
        jax.ShapeDtypeStruct((n, IDX_DIM), F32),
        jax.ShapeDtypeStruct((n, IDX_DIM), BF16),
        jax.ShapeDtypeStruct((n, N_IDX_HEADS), F32),
    )
    out_specs = (row(aw), row(aw), row(aw), row(aw), pl.BlockSpec((aw, tm), lambda i: (0, i)), row(aw),
                 row(IDX_DIM), row(IDX_DIM), row(N_IDX_HEADS))
    return pl.pallas_call(
        _inproj_kernel,
        grid=(n // tm,),
        in_specs=[row(d), const(g_mix), const(w_main), const(w_small),
                  row(LANES), row(LANES), row(LANES)],
        out_specs=out_specs,
        out_shape=out_shapes,
        compiler_params=_cparams(("arbitrary",)),
        name="in_projection",
    )(x2d, g_mix, w_main, w_small, *tabs)


HIST = 32


def _conv_tail(dw, bdw, lng, lnb, wco, bco):
    dw = dw + bdw
    mu = jnp.mean(dw, axis=-1, keepdims=True)
    cen = dw - mu
    var = jnp.mean(cen * cen, axis=-1, keepdims=True)
    y = cen * lax.rsqrt(var + NORM_EPS) * lng + lnb
    hh = y * jax.nn.sigmoid(y)
    return _dot(hh.astype(BF16), wco) + bco


def _depthwise(pad_ref, wdw_ref, rows):
    base = HIST - (CONV_WIDTH - 1)
    acc = pad_ref[pl.ds(base, rows), :] * wdw_ref[0:1, :]
    for j in range(1, CONV_WIDTH):
        acc = acc + pad_ref[pl.ds(base + j, rows), :] * wdw_ref[j:j + 1, :]
    return acc


def _conv_prompt_kernel(x_ref, g_ref, wcu_ref, wdw_ref, bdw_ref, lng_ref, lnb_ref, wco_ref, bco_ref,
                        y_ref, st_ref, pad_ref):
    t = pl.program_id(1)
    tm = x_ref.shape[0]
    dc = wdw_ref.shape[1]

    @pl.when(t == 0)
    def _():
        pad_ref[0:HIST, :] = jnp.zeros((HIST, dc), F32)

    xn = _rms(x_ref[...], g_ref[...]).astype(BF16)
    cu = _dot(xn, wcu_ref[...])
    glu = cu[:, 0:dc] * jax.nn.sigmoid(cu[:, dc:2 * dc])
    pad_ref[HIST:HIST + tm, :] = glu
    dw = _depthwise(pad_ref, wdw_ref, tm)
    y_ref[...] = _conv_tail(dw, bdw_ref[...], lng_ref[...], lnb_ref[...], wco_ref[...], bco_ref[...])
    tail = pad_ref[tm:tm + HIST, :]
    pad_ref[0:HIST, :] = tail

    @pl.when(t == pl.num_programs(1) - 1)
    def _():
        st_ref[...] = tail[HIST - (CONV_WIDTH - 1):, :]


def _conv_prompt(x2d, nb, g_mix, wcu, wdw, bdw, lng, lnb, wco, bco, tm):
    n, d = x2d.shape
    t = n // nb
    dc = wdw.shape[1]
    nt = t // tm
    const = lambda a: pl.BlockSpec(a.shape, lambda b, i: (0,) * a.ndim)
    return pl.pallas_call(
        _conv_prompt_kernel,
        grid=(nb, nt),
        in_specs=[pl.BlockSpec((tm, d), lambda b, i: (b * nt + i, 0)),
                  const(g_mix), const(wcu), const(wdw), const(bdw), const(lng), const(lnb),
                  const(wco), const(bco)],
        out_specs=(pl.BlockSpec((tm, d), lambda b, i: (b * nt + i, 0)),
                   pl.BlockSpec((None, CONV_WIDTH - 1, dc), lambda b, i: (b, 0, 0))),
        out_shape=(jax.ShapeDtypeStruct((n, d), F32),
                   jax.ShapeDtypeStruct((nb, CONV_WIDTH - 1, dc), F32)),
        scratch_shapes=[pltpu.VMEM((HIST + tm, dc), F32)],
        compiler_params=_cparams(("arbitrary", "arbitrary")),
        name="conv_prompt",
    )(x2d, g_mix, wcu, wdw, bdw, lng, lnb, wco, bco)


def _conv_sample_kernel(x_ref, st_ref, g_ref, wcu_ref, wdw_ref, bdw_ref, lng_ref, lnb_ref, wco_ref, bco_ref,
                        y_ref, nst_ref, pad_ref, glu_ref, dw_ref, *, ds):
    dc = wdw_ref.shape[1]
    nseq = st_ref.shape[0]
    w1 = CONV_WIDTH - 1
    xn = _rms(x_ref[...], g_ref[...]).astype(BF16)
    cu = _dot(xn, wcu_ref[...])
    glu_ref[...] = cu[:, 0:dc] * jax.nn.sigmoid(cu[:, dc:2 * dc])

    def body(b, carry):
        r0 = pl.multiple_of(b * ds, ds)
        pad_ref[HIST - w1:HIST, :] = st_ref[b]
        pad_ref[HIST:HIST + ds, :] = glu_ref[pl.ds(r0, ds), :]
        dw_ref[pl.ds(r0, ds), :] = _depthwise(pad_ref, wdw_ref, ds)
        nst_ref[b] = pad_ref[HIST + ds - w1:HIST + ds, :]
        return carry

    lax.fori_loop(0, nseq, body, 0)
    y_ref[...] = _conv_tail(dw_ref[...], bdw_ref[...], lng_ref[...], lnb_ref[...], wco_ref[...], bco_ref[...])


def _conv_sample(x2d, state, g_mix, wcu, wdw, bdw, lng, lnb, wco, bco, ds):
    n, d = x2d.shape
    dc = wdw.shape[1]
    full = lambda a: pl.BlockSpec(a.shape, lambda i: (0,) * a.ndim)
    args = (x2d, state, g_mix, wcu, wdw, bdw, lng, lnb, wco, bco)
    return pl.pallas_call(
        functools.partial(_conv_sample_kernel, ds=ds),
        grid=(1,),
        in_specs=[full(a) for a in args],
        out_specs=(pl.BlockSpec((n, d), lambda i: (0, 0)),
                   pl.BlockSpec(state.shape, lambda i: (0, 0, 0))),
        out_shape=(jax.ShapeDtypeStruct((n, d), F32),
                   jax.ShapeDtypeStruct(state.shape, F32)),
        scratch_shapes=[pltpu.VMEM((HIST + ds, dc), F32), pltpu.VMEM((n, dc), F32),
                        pltpu.VMEM((n, dc), F32)],
        compiler_params=_cparams(("arbitrary",)),
        name="conv_sample",
    )(*args)


def _kth_largest_key(count_ge, shape, keep):
    def bit_body(it, th_u):
        cand_u = th_u | lax.shift_left(jnp.int32(1), 31 - it)
        cnt = count_ge(_key_to_float(cand_u ^ jnp.int32(INT_MIN)))
        return jnp.where(cnt >= keep, cand_u, th_u)

    th_u = lax.fori_loop(0, 32, bit_body, jnp.zeros(shape, I32))
    return th_u ^ jnp.int32(INT_MIN)


def _prompt_attn_kernel(qi_ref, wi_ref, ki_ref, q_ref, k_ref, vt_ref, o_ref,
                        sc_ref, lg_ref, *acc_refs, ch, keep):
    qb = Q_BLOCK
    i = pl.program_id(1)
    t0 = i * qb
    n_ch = (t0 + qb + ch - 1) // ch
    n_pairs = N_HEADS // 2
    int_min = jnp.int32(INT_MIN)

    qi = qi_ref[...]
    wi = wi_ref[...]
    qi_pairs = [jnp.concatenate([qi[:, (2 * p) * IDX_DIM:(2 * p + 1) * IDX_DIM],
                                 qi[:, (2 * p + 1) * IDX_DIM:(2 * p + 2) * IDX_DIM]], axis=0)
                for p in range(N_IDX_HEADS // 2)]
    s_idx = lax.broadcasted_iota(I32, (ch, qb), 0)
    t_idx = t0 + lax.broadcasted_iota(I32, (ch, qb), 1)

    def score_body(c, carry):
        s0 = pl.multiple_of(c * ch, ch)
        kic = ki_ref[pl.ds(s0, ch), :]
        acc = None
        for p in range(N_IDX_HEADS // 2):
            d = _dot_nt(kic, qi_pairs[p])
            for u in range(2):
                h = 2 * p + u
                term = jnp.maximum(d[:, u * qb:(u + 1) * qb], 0.0) * wi[h:h + 1, :]
                acc = term if acc is None else acc + term
        sc_ref[c] = jnp.where(s0 + s_idx <= t_idx, acc * IDX_HEAD_SCALE, -jnp.inf)
        return carry

    lax.fori_loop(0, n_ch, score_body, 0)

    def count_ge(cand):
        def body(c, acc):
            m = (sc_ref[c] >= cand).astype(I32)
            return acc + jnp.sum(m.reshape(ch // SUBLANES, SUBLANES, qb), axis=0)
        acc = lax.fori_loop(0, n_ch, body, jnp.zeros((SUBLANES, qb), I32))
        return jnp.sum(acc, axis=0, keepdims=True)

    def bis_cond(st):
        it, _, cnt_th = st
        return (it < 32) & (jnp.max(jnp.where(cnt_th == keep, 0, 1)) > 0)

    def bis_body(st):
        it, th_u, cnt_th = st
        for j in range(BITS_PER_CHECK):
            cand_u = th_u | lax.shift_left(jnp.int32(1), 31 - (it + j))
            cnt = count_ge(_key_to_float(cand_u ^ int_min))
            ok = cnt >= keep
            th_u = jnp.where(ok, cand_u, th_u)
            cnt_th = jnp.where(ok, cnt, cnt_th)
        return it + BITS_PER_CHECK, th_u, cnt_th

    _, th_u, cnt_th = lax.while_loop(
        bis_cond, bis_body,
        (jnp.int32(0), jnp.zeros((1, qb), I32), jnp.zeros((1, qb), I32) + n_ch * ch))
    th_key = th_u ^ int_min
    th = _key_to_float(th_key)
    need_fix = (cnt_th > keep) & (th_key > jnp.int32(KEY_NEG_INF))

    @pl.when(jnp.max(need_fix.astype(I32)) > 0)
    def _():
        need = (keep - count_ge(_key_to_float(th_key + 1))).astype(F32)
        tri = (lax.broadcasted_iota(I32, (ch, ch), 1) <= lax.broadcasted_iota(I32, (ch, ch), 0)).astype(BF16)

        def body(c, run):
            sc = sc_ref[c]
            eq = (sc == th) & need_fix
            eqf = jnp.where(eq, 1.0, 0.0)
            pos = run + _dot(tri, eqf.astype(BF16))
            sc_ref[c] = jnp.where(eq & (pos > need), -jnp.inf, sc)
            return run + jnp.sum(eqf, axis=0, keepdims=True)

        lax.fori_loop(0, n_ch, body, jnp.zeros((1, qb), F32))

    th_sel = jnp.maximum(th, -F32_MAX)

    q = q_ref[...]
    lo = lax.broadcasted_iota(I32, (qb, LANES), 1) < HEAD_DIM
    qm_pairs = []
    for p in range(n_pairs):
        qp = q[:, p * LANES:(p + 1) * LANES]
        zero = jnp.zeros_like(qp)
        qm_pairs.append(jnp.concatenate([jnp.where(lo, qp, zero), jnp.where(lo, zero, qp)], axis=0))
    for acc_ref in acc_refs:
        acc_ref[...] = jnp.zeros(acc_ref.shape, F32)

    def logits_into(c, slot):
        s0 = pl.multiple_of(c * ch, ch)
        for p in range(n_pairs):
            kp = k_ref[pl.ds(s0, ch), p * LANES:(p + 1) * LANES]
            lg_ref[slot, p] = _dot_nt(kp, qm_pairs[p])

    logits_into(0, 0)

    def attn_body(c, carry):
        ms, ls = carry
        s0 = pl.multiple_of(c * ch, ch)
        slot = c % 2
        logits_into(jnp.minimum(c + 1, n_ch - 1), 1 - slot)
        bias = jnp.where(sc_ref[c] >= th_sel, 0.0, NEG_BIG)
        nms, nls = [], []
        for p in range(n_pairs):
            for u in range(2):
                h = 2 * p + u
                lg = lg_ref[slot, p, :, u * qb:(u + 1) * qb] + bias
                m_new = jnp.maximum(ms[h], jnp.max(lg, axis=0, keepdims=True))
                alpha = jnp.exp(ms[h] - m_new)
                pe = jnp.exp(lg - m_new)
                nls.append(ls[h] * alpha + jnp.sum(pe, axis=0, keepdims=True))
                nms.append(m_new)
                vth = vt_ref[h * HEAD_DIM:(h + 1) * HEAD_DIM, pl.ds(s0, ch)]
                acc_refs[h][...] = acc_refs[h][...] * alpha + _dot(vth, pe.astype(BF16))
        return tuple(nms), tuple(nls)

    init = (tuple(jnp.full((1, qb), NEG_BIG, F32) for _ in range(N_HEADS)),
            tuple(jnp.zeros((1, qb), F32) for _ in range(N_HEADS)))
    _, ls = lax.fori_loop(0, n_ch, attn_body, init)
    outs = []
    for p in range(n_pairs):
        ot = jnp.concatenate([acc_refs[2 * p][...] / ls[2 * p],
                              acc_refs[2 * p + 1][...] / ls[2 * p + 1]], axis=0)
        outs.append(ot.T)
    o_ref[...] = jnp.concatenate(outs, axis=1).astype(BF16)


def _prompt_attention(qi, wi_t, ki, q, k, vt, nb, ch):
    n = q.shape[0]
    t = n // nb
    nq = t // Q_BLOCK
    keep = min(TOPK_MAX, t // 4)
    aw = ATTN_WIDTH
    qblk = lambda w: pl.BlockSpec((Q_BLOCK, w), lambda b, i: (b * nq + i, 0))
    batch = lambda w: pl.BlockSpec((t, w), lambda b, i: (b, 0), pipeline_mode=pl.Buffered(1))
    return pl.pallas_call(
        functools.partial(_prompt_attn_kernel, ch=ch, keep=keep),
        grid=(nb, nq),
        in_specs=[qblk(aw), pl.BlockSpec((N_IDX_HEADS, Q_BLOCK), lambda b, i: (0, b * nq + i)),
                  batch(IDX_DIM), qblk(aw), batch(aw),
                  pl.BlockSpec((aw, t), lambda b, i: (0, b), pipeline_mode=pl.Buffered(1))],
        out_specs=qblk(aw),
        out_shape=jax.ShapeDtypeStruct((n, aw), BF16),
        scratch_shapes=[pltpu.VMEM((t // ch, ch, Q_BLOCK), F32),
                        pltpu.VMEM((2, N_HEADS // 2, ch, 2 * Q_BLOCK), F32)]
                       + [pltpu.VMEM((HEAD_DIM, Q_BLOCK), F32) for _ in range(N_HEADS)],
        compiler_params=_cparams(("arbitrary", "arbitrary")),
        name="prompt_attention",
    )(qi, wi_t, ki, q, k, vt)


def _sample_score_kernel(pt_ref, a_ref, w_ref, *refs, pg, ds):
    pages = refs[:pg]
    kin_ref, keys_ref, keysn_ref = refs[pg:pg + 3]
    g = pl.program_id(1)
    a = a_ref[...]
    w = w_ref[...]

    def score_page(kpage_t):
        r = jnp.maximum(_dot(a, kpage_t), 0.0) * w
        s = r[0:ds]
        for h in range(1, N_IDX_HEADS):
            s = s + r[h * ds:(h + 1) * ds]
        return s * IDX_HEAD_SCALE

    for j in range(pg):
        keys_ref[:, j * PAGE_SIZE:(j + 1) * PAGE_SIZE] = score_page(pages[j][...].astype(BF16))

    @pl.when(g == pl.num_programs(1) - 1)
    def _():
        sn = score_page(kin_ref[...])
        row = lax.broadcasted_iota(I32, sn.shape, 0)
        colv = lax.broadcasted_iota(I32, sn.shape, 1)
        keysn_ref[...] = jnp.where(colv <= row, sn, -jnp.inf)


def _sample_scores(page_table, a_mat, w_col, cache_kidx, ki_new, pg, ds):
    db, npages = page_table.shape
    ng = npages // pg
    page_spec = lambda j: pl.BlockSpec((None, IDX_DIM, PAGE_SIZE),
                                       lambda b, g, pt: (pt[b, g * pg + j], 0, 0))
    grid_spec = pltpu.PrefetchScalarGridSpec(
        num_scalar_prefetch=1,
        grid=(db, ng),
        in_specs=[pl.BlockSpec((None,) + a_mat.shape[1:], lambda b, g, pt: (b, 0, 0)),
                  pl.BlockSpec((None,) + w_col.shape[1:], lambda b, g, pt: (b, 0, 0))]
                 + [page_spec(j) for j in range(pg)]
                 + [pl.BlockSpec((None, IDX_DIM, PAGE_SIZE), lambda b, g, pt: (b, 0, 0))],
        out_specs=(pl.BlockSpec((None, ds, pg * PAGE_SIZE), lambda b, g, pt: (b, 0, g)),
                   pl.BlockSpec((None, ds, PAGE_SIZE), lambda b, g, pt: (b, 0, 0))),
    )
    return pl.pallas_call(
        functools.partial(_sample_score_kernel, pg=pg, ds=ds),
        grid_spec=grid_spec,
        out_shape=(jax.ShapeDtypeStruct((db, ds, npages * PAGE_SIZE), F32),
                   jax.ShapeDtypeStruct((db, ds, PAGE_SIZE), F32)),
        compiler_params=_cparams(("arbitrary", "arbitrary")),
        name="sample_scores",
    )(page_table, a_mat, w_col, *([cache_kidx] * pg), ki_new)


def _sample_thresh_kernel(keys_ref, keysn_ref, th_ref, okeys_ref, okeysn_ref, *, keep):
    ds, past = keys_ref.shape
    nblk = past // LANES
    okeys_ref[...] = keys_ref[...]
    okeysn_ref[...] = keysn_ref[...]

    def count_ge(cand):
        acc = _fold_lanes((okeys_ref[...] >= cand).astype(I32)) + (okeysn_ref[...] >= cand).astype(I32)
        return jnp.sum(acc, axis=-1, keepdims=True)

    th_key = _kth_largest_key(count_ge, (ds, 1), keep)
    th = _key_to_float(th_key)
    cnt_ge = count_ge(th)
    need_fix = (cnt_ge > keep) & (th_key > jnp.int32(KEY_NEG_INF))

    @pl.when(jnp.max(need_fix.astype(I32)) > 0)
    def _():
        need = (keep - count_ge(_key_to_float(th_key + 1))).astype(F32)
        tri = (lax.broadcasted_iota(I32, (LANES, LANES), 0)
               <= lax.broadcasted_iota(I32, (LANES, LANES), 1)).astype(BF16)

        def fix(kc, run):
            eq = (kc == th) & need_fix
            eqf = jnp.where(eq, 1.0, 0.0)
            pos = run + _dot(eqf.astype(BF16), tri)
            return jnp.where(eq & (pos > need), -jnp.inf, kc), run + jnp.sum(eqf, axis=-1, keepdims=True)

        def body(j, run):
            s0 = pl.multiple_of(j * LANES, LANES)
            kc, run = fix(okeys_ref[:, pl.ds(s0, LANES)], run)
            okeys_ref[:, pl.ds(s0, LANES)] = kc
            return run

        run = lax.fori_loop(0, nblk, body, jnp.zeros((ds, 1), F32))
        kc, _ = fix(okeysn_ref[...], run)
        okeysn_ref[...] = kc

    th_ref[...] = jnp.broadcast_to(jnp.maximum(th, -F32_MAX), th_ref.shape)


def _sample_threshold(keys, keysn, keep):
    db, ds, past = keys.shape
    blk = lambda w: pl.BlockSpec((None, ds, w), lambda b: (b, 0, 0))
    return pl.pallas_call(
        functools.partial(_sample_thresh_kernel, keep=keep),
        grid=(db,),
        in_specs=[blk(past), blk(PAGE_SIZE)],
        out_specs=(blk(LANES), blk(past), blk(PAGE_SIZE)),
        out_shape=(jax.ShapeDtypeStruct((db, ds, LANES), F32),
                   jax.ShapeDtypeStruct((db, ds, past), F32),
                   jax.ShapeDtypeStruct((db, ds, PAGE_SIZE), F32)),
        compiler_params=_cparams(("arbitrary",)),
        name="sample_threshold",
    )(keys, keysn)


def _sample_attn_kernel(pt_ref, qbd_ref, th_ref, keys_ref, keysn_ref, kn_ref, vn_ref, *refs, pg, ds):
    kpages = refs[:pg]
    vpages = refs[pg:2 * pg]
    o_ref, m_ref, l_ref, acc_ref = refs[2 * pg:2 * pg + 4]
    g = pl.program_id(1)
    rows = N_HEADS * ds

    @pl.when(g == 0)
    def _():
        m_ref[...] = jnp.full(m_ref.shape, NEG_BIG, F32)
        l_ref[...] = jnp.zeros(l_ref.shape, F32)
        acc_ref[...] = jnp.zeros(acc_ref.shape, F32)

    qbd = qbd_ref[...]
    th = th_ref[...]

    def attend(kpages_t, vpages_t, scores):
        lgs = []
        for kt, sc in zip(kpages_t, scores):
            lg = _dot(qbd, kt).reshape(N_HEADS, ds, PAGE_SIZE)
            lgs.append(jnp.where((sc >= th)[None], lg, NEG_BIG).reshape(rows, PAGE_SIZE))
        m_blk = lgs[0]
        for lg in lgs[1:]:
            m_blk = jnp.maximum(m_blk, lg)
        m_old = m_ref[...]
        m_new = jnp.maximum(m_old, jnp.max(m_blk, axis=-1, keepdims=True))
        alpha = jnp.exp(m_old - m_new)
        pes = [jnp.exp(lg - m_new) for lg in lgs]
        s_blk = pes[0]
        for pe in pes[1:]:
            s_blk = s_blk + pe
        pv = None
        for pe, vt in zip(pes, vpages_t):
            term = _dot_nt(pe.astype(BF16), vt)
            pv = term if pv is None else pv + term
        l_ref[...] = l_ref[...] * alpha + jnp.sum(s_blk, axis=-1, keepdims=True)
        acc_ref[...] = acc_ref[...] * alpha + pv
        m_ref[...] = m_new

    attend([kpages[j][...].astype(BF16) for j in range(pg)],
           [vpages[j][...].astype(BF16) for j in range(pg)],
           [keys_ref[:, j * PAGE_SIZE:(j + 1) * PAGE_SIZE] for j in range(pg)])

    @pl.when(g == pl.num_programs(1) - 1)
    def _():
        attend([kn_ref[...]], [vn_ref[...]], [keysn_ref[...]])
        res = acc_ref[...] / l_ref[...]
        head = lax.broadcasted_iota(I32, (ds, ATTN_WIDTH), 1) // HEAD_DIM
        out = jnp.zeros((ds, ATTN_WIDTH), F32)
        for h in range(N_HEADS):
            out = out + jnp.where(head == h, res[h * ds:(h + 1) * ds, :], 0.0)
        o_ref[...] = out.astype(BF16)


def _sample_attention(page_table, qbd, th, keys, keysn, k_new, v_new, cache_k, cache_v, pg, ds):
    db, npages = page_table.shape
    ng = npages // pg
    aw = ATTN_WIDTH
    rows = N_HEADS * ds
    per_b = lambda shp: pl.BlockSpec((None,) + shp, lambda b, g, pt: (b, 0, 0))
    page_spec = lambda j: pl.BlockSpec((None, aw, PAGE_SIZE), lambda b, g, pt: (pt[b, g * pg + j], 0, 0))
    grid_spec = pltpu.PrefetchScalarGridSpec(
        num_scalar_prefetch=1,
        grid=(db, ng),
        in_specs=[per_b((rows, aw)), per_b((ds, LANES)),
                  pl.BlockSpec((None, ds, pg * PAGE_SIZE), lambda b, g, pt: (b, 0, g)),
                  per_b((ds, PAGE_SIZE)), per_b((aw, PAGE_SIZE)), per_b((aw, PAGE_SIZE))]
                 + [page_spec(j) for j in range(pg)] + [page_spec(j) for j in range(pg)],
        out_specs=per_b((ds, aw)),
        scratch_shapes=[pltpu.VMEM((rows, 1), F32), pltpu.VMEM((rows, 1), F32),
                        pltpu.VMEM((rows, aw), F32)],
    )
    return pl.pallas_call(
        functools.partial(_sample_attn_kernel, pg=pg, ds=ds),
        grid_spec=grid_spec,
        out_shape=jax.ShapeDtypeStruct((db, ds, aw), BF16),
        compiler_params=_cparams(("arbitrary", "arbitrary")),
        name="sample_attention",
    )(page_table, qbd, th, keys, keysn, k_new, v_new, *([cache_k] * pg), *([cache_v] * pg))


def _merge_kernel(x_ref, o_ref, yc_ref, gmix_ref, wgt_ref, wao_ref, wmo_ref, gffn_ref, wr_ref, br_ref,
                  x1_ref, xn2_ref, ti_ref, tw_ref):
    x = x_ref[...]
    d = x.shape[1]
    xn = _rms(x, gmix_ref[...]).astype(BF16)
    gate = jax.nn.sigmoid(_dot(xn, wgt_ref[...]))
    ya = _dot(o_ref[...], wao_ref[...])
    mix = gate[:, 0:d] * ya + gate[:, d:2 * d] * yc_ref[...]
    x1 = x + _dot(mix.astype(BF16), wmo_ref[...])
    x1_ref[...] = x1
    xn2 = _rms(x1, gffn_ref[...]).astype(BF16)
    xn2_ref[...] = xn2
    logits = _dot(xn2, wr_ref[...]) + br_ref[...]
    lane = lax.broadcasted_iota(I32, logits.shape, 1)
    vals, idxs = [], []
    for _ in range(TOP_K):
        m = jnp.max(logits, axis=-1, keepdims=True)
        idx = jnp.min(jnp.where(logits == m, lane, LANES), axis=-1, keepdims=True)
        vals.append(m)
        idxs.append(idx)
        logits = jnp.where(lane == idx, -jnp.inf, logits)
    es = [jnp.exp(v - vals[0]) for v in vals]
    den = es[0]
    for e in es[1:]:
        den = den + e
    tw = jnp.zeros(logits.shape, F32)
    ti = jnp.zeros(logits.shape, I32)
    for j in range(TOP_K):
        tw = jnp.where(lane == j, es[j] / den, tw)
        ti = jnp.where(lane == j, idxs[j], ti)
    tw_ref[...] = tw
    ti_ref[...] = ti


def _merge(x2d, o, yc, g_mix, wgt, wao, wmo, g_ffn, wr, br, tm):
    n, d = x2d.shape
    row = lambda w: pl.BlockSpec((tm, w), lambda i: (i, 0))
    const = lambda a: pl.BlockSpec(a.shape, lambda i: (0, 0))
    return pl.pallas_call(
        _merge_kernel,
        grid=(n // tm,),
        in_specs=[row(d), row(ATTN_WIDTH), row(d), const(g_mix), const(wgt), const(wao), const(wmo),
                  const(g_ffn), const(wr), const(br)],
        out_specs=(row(d), row(d), row(LANES), row(LANES)),
        out_shape=(jax.ShapeDtypeStruct((n, d), F32), jax.ShapeDtypeStruct((n, d), BF16),
                   jax.ShapeDtypeStruct((n, LANES), I32), jax.ShapeDtypeStruct((n, LANES), F32)),
        compiler_params=_cparams(("arbitrary",)),
        name="merge_router",
    )(x2d, o, yc, g_mix, wgt, wao, wmo, g_ffn, wr, br)


def _moe_kernel(te_ref, nu_ref, xs_ref, wg_ref, bg_ref, wu_ref, bu_ref, wd_ref, bd_ref, y_ref,
                wgb_ref, wub_ref, wdb_ref):
    i = pl.program_id(0)
    active = i < nu_ref[0]
    new_expert = (i == 0) | (te_ref[i] != te_ref[jnp.maximum(i - 1, 0)])

    @pl.when(active & new_expert)
    def _():
        wgb_ref[...] = wg_ref[...].astype(BF16)
        wub_ref[...] = wu_ref[...].astype(BF16)
        wdb_ref[...] = wd_ref[...].astype(BF16)

    @pl.when(active)
    def _():
        x = xs_ref[...]
        gate = jnp.minimum(_dot(x, wgb_ref[...]) + bg_ref[...], SWIGLU_LIMIT)
        up = jnp.clip(_dot(x, wub_ref[...]) + bu_ref[...], -SWIGLU_LIMIT, SWIGLU_LIMIT)
        act = (up + 1.0) * gate * jax.nn.sigmoid(SWIGLU_ALPHA * gate)
        y_ref[...] = _dot(act.astype(BF16), wdb_ref[...]) + bd_ref[...]

    @pl.when(jnp.logical_not(active))
    def _():
        y_ref[...] = jnp.zeros(y_ref.shape, F32)


def _moe_experts(tile_expert, n_used, xs, w_gate, b_gate, w_up, b_up, w_down, b_down, tme):
    r, d = xs.shape
    ne, _, f = w_gate.shape
    wspec = lambda k, n: pl.BlockSpec((None, k, n), lambda i, te, nu: (te[i], 0, 0))
    bspec = lambda n: pl.BlockSpec((None, 1, n), lambda i, te, nu: (te[i], 0, 0))
    grid_spec = pltpu.PrefetchScalarGridSpec(
        num_scalar_prefetch=2,
        grid=(r // tme,),
        in_specs=[pl.BlockSpec((tme, d), lambda i, te, nu: (i, 0)),
                  wspec(d, f), bspec(f), wspec(d, f), bspec(f), wspec(f, d), bspec(d)],
        out_specs=pl.BlockSpec((tme, d), lambda i, te, nu: (i, 0)),
        scratch_shapes=[pltpu.VMEM((d, f), BF16), pltpu.VMEM((d, f), BF16), pltpu.VMEM((f, d), BF16)],
    )
    return pl.pallas_call(
        _moe_kernel,
        grid_spec=grid_spec,
        out_shape=jax.ShapeDtypeStruct((r, d), F32),
        compiler_params=_cparams(("arbitrary",)),
        name="moe_experts",
    )(tile_expert, n_used, xs, w_gate, b_gate.reshape(ne, 1, f), w_up, b_up.reshape(ne, 1, f),
      w_down, b_down.reshape(ne, 1, d))


def _final_kernel(x1_ref, yg_ref, tw_ref, g_ref, y_ref):
    tw = tw_ref[...]
    moe = tw[:, 0:1] * yg_ref[0]
    for j in range(1, TOP_K):
        moe = moe + tw[:, j:j + 1] * yg_ref[j]
    y_ref[...] = _rms(x1_ref[...] + moe, g_ref[...])


def _final(x1, yg, tw, g_final, row0, tm):
    n, d = x1.shape
    assert row0 % tm == 0
    off = row0 // tm
    return pl.pallas_call(
        _final_kernel,
        grid=(n // tm,),
        in_specs=[pl.BlockSpec((tm, d), lambda i: (i, 0)),
                  pl.BlockSpec((TOP_K, tm, d), lambda i: (0, off + i, 0)),
                  pl.BlockSpec((tm, LANES), lambda i: (off + i, 0)),
                  pl.BlockSpec(g_final.shape, lambda i: (0, 0))],
        out_specs=pl.BlockSpec((tm, d), lambda i: (i, 0)),
        out_shape=jax.ShapeDtypeStruct((n, d), F32),
        compiler_params=_cparams(("arbitrary",)),
        name="combine_final_norm",
    )(x1, yg, tw, g_final)


def _rope_tables(pos):
    half = ROT_DIM // 2
    inv_freq = jnp.exp(-math.log(ROPE_THETA) * jnp.arange(half, dtype=F32) * (2.0 / ROT_DIM))
    ang = pos.astype(F32)[:, None] * inv_freq[None, :]
    cos, sin = jnp.cos(ang), jnp.sin(ang)
    n = pos.shape[0]
    rest = HEAD_DIM - ROT_DIM
    z8 = jnp.zeros((n, half), F32)
    zr = jnp.zeros((n, rest), F32)
    c = jnp.concatenate([cos, cos, jnp.ones((n, rest), F32)], axis=1)
    s1 = jnp.concatenate([-sin, z8, zr], axis=1)
    s2 = jnp.concatenate([z8, sin, zr], axis=1)
    rep = LANES // HEAD_DIM
    return tuple(jnp.tile(a, (1, rep)) for a in (c, s1, s2))


def _moe_plan(top_i, n_experts, tme):
    p = top_i.size
    flat_e = top_i.reshape(-1)
    onehot = (flat_e[:, None] == jnp.arange(n_experts, dtype=I32)[None, :]).astype(I32)
    csum = jnp.cumsum(onehot, axis=0)
    rank = jnp.sum((csum - onehot) * onehot, axis=1)
    counts = csum[-1]
    padded = ((counts + tme - 1) // tme) * tme
    ends = jnp.cumsum(padded)
    offs = ends - padded
    dest = offs[flat_e] + rank
    n_rows = ((p + n_experts * (tme - 1)) // tme) * tme
    src_tok = jnp.zeros((n_rows,), I32).at[dest].set(jnp.arange(p, dtype=I32) // TOP_K)
    n_tiles = n_rows // tme
    tile_start = jnp.arange(n_tiles, dtype=I32) * tme
    tile_expert = jnp.minimum(jnp.sum((ends[None, :] <= tile_start[:, None]).astype(I32), axis=1), n_experts - 1)
    n_used = (ends[-1] // tme).astype(I32).reshape(1)
    last_e = tile_expert[jnp.maximum(n_used[0] - 1, 0)]
    tile_expert = jnp.where(jnp.arange(n_tiles) < n_used[0], tile_expert, last_e)
    return src_tok, dest, tile_expert, n_used


def kernel(x_prompt, x_sample, cache_k, cache_v, cache_kidx, state_conv, page_table, g_mix, w_in, w_attn_out,
           w_dw, b_dw, ln_g, ln_b, w_conv_out, b_conv_out, w_mix_out, g_ffn, w_router, b_router, w_gate,
           b_gate, w_up, b_up, w_down, b_down, g_final, *, attn_chunk=512, tile_rows=256, conv_rows=512,
           moe_rows=256, page_group=16):
    nb, t, d = x_prompt.shape
    db, ds, _ = x_sample.shape
    depth = g_mix.shape[0]
    assert depth == 1 and ds == SUBLANES
    n_pool = cache_k.shape[1]
    npages = page_table.shape[1]
    past = npages * PAGE_SIZE
    aw = ATTN_WIDTH
    dc = d // 2
    n_experts = w_router.shape[-1]
    np_, ns = nb * t, db * ds

    w_in0 = w_in[0]
    c_small = 4 * aw
    c_cu = c_small + IDX_DIM + N_IDX_HEADS
    c_gt = c_cu + 2 * dc
    w_main = w_in0[:, :c_small].astype(BF16)
    w_small = jnp.pad(w_in0[:, c_small:c_cu], ((0, 0), (0, LANES - (c_cu - c_small)))).astype(BF16)
    w_cu = w_in0[:, c_cu:c_gt].astype(BF16)
    w_gt = w_in0[:, c_gt:].astype(BF16)
    w_ao = w_attn_out[0].astype(BF16)
    w_co = w_conv_out[0].astype(BF16)
    w_mo = w_mix_out[0].astype(BF16)
    w_r = jnp.pad(w_router[0], ((0, 0), (0, LANES - n_experts))).astype(BF16)
    b_r = jnp.pad(b_router[0], (0, LANES - n_experts), constant_values=NEG_BIG).reshape(1, LANES)
    gm = g_mix[0].reshape(1, d)
    gf = g_ffn[0].reshape(1, d)
    row = lambda a: a.reshape(1, -1)
    conv_w = (w_dw[0], row(b_dw[0]), row(ln_g[0]), row(ln_b[0]), w_co, row(b_conv_out[0]))

    xp = x_prompt.reshape(np_, d)
    xs = x_sample.reshape(ns, d)
    pos_p = jnp.tile(jnp.arange(t, dtype=I32), nb)
    pos_s = jnp.tile(past + jnp.arange(ds, dtype=I32), db)

    q_p, kf_p, kb_p, vf_p, vt_p, qi_p, kif_p, kib_p, wi_p = _in_projection(
        xp, gm, w_main, w_small, _rope_tables(pos_p), tile_rows)
    o_p = _prompt_attention(qi_p, wi_p.T, kib_p, q_p, kb_p, vt_p, nb, min(attn_chunk, t))
    yc_p, conv_p = _conv_prompt(xp, nb, gm, w_cu, *conv_w, min(conv_rows, t))
    x1_p, xn2_p, ti_p, tw_p = _merge(xp, o_p, yc_p, gm, w_gt, w_ao, w_mo, gf, w_r, b_r, tile_rows)

    q_s, kf_s, kb_s, vf_s, vt_s, qi_s, kif_s, kib_s, wi_s = _in_projection(
        xs, gm, w_main, w_small, _rope_tables(pos_s), ns)
    a_mat = qi_s.reshape(db, ds, N_IDX_HEADS, IDX_DIM).transpose(0, 2, 1, 3).reshape(db, N_IDX_HEADS * ds, IDX_DIM)
    w_col = wi_s.reshape(db, ds, N_IDX_HEADS).transpose(0, 2, 1).reshape(db, N_IDX_HEADS * ds, 1)
    page_t = lambda a: jnp.pad(a.reshape(db, ds, -1).transpose(0, 2, 1), ((0, 0), (0, 0), (0, PAGE_SIZE - ds)))
    kidx_t = cache_kidx[0].transpose(0, 2, 1)
    ck_t = cache_k[0].transpose(0, 2, 3, 1).reshape(n_pool, aw, PAGE_SIZE)
    cv_t = cache_v[0].transpose(0, 2, 3, 1).reshape(n_pool, aw, PAGE_SIZE)
    vn_t = jnp.pad(vt_s.reshape(aw, db, ds).transpose(1, 0, 2), ((0, 0), (0, 0), (0, PAGE_SIZE - ds)))
    keys, keysn = _sample_scores(page_table, a_mat, w_col, kidx_t, page_t(kib_s), page_group, ds)
    keep_s = min(TOPK_MAX, (past + ds) // 4)
    th, keys, keysn = _sample_threshold(keys, keysn, keep_s)
    eye = jnp.eye(N_HEADS, dtype=BF16)
    qbd = (q_s.reshape(db, ds, N_HEADS, 1, HEAD_DIM).transpose(0, 2, 1, 3, 4)
           * eye[None, :, None, :, None]).reshape(db, N_HEADS * ds, aw)
    o_s = _sample_attention(page_table, qbd, th, keys, keysn, page_t(kb_s), vn_t, ck_t, cv_t,
                            page_group, ds).reshape(ns, aw)
    yc_s, conv_s = _conv_sample(xs, state_conv[0], gm, w_cu, *conv_w, ds)
    x1_s, xn2_s, ti_s, tw_s = _merge(xs, o_s, yc_s, gm, w_gt, w_ao, w_mo, gf, w_r, b_r, ns)

    xn2 = jnp.concatenate([xn2_p, xn2_s], axis=0)
    top_i = jnp.concatenate([ti_p, ti_s], axis=0)[:, :TOP_K]
    top_w = jnp.concatenate([tw_p, tw_s], axis=0)
    src_tok, dest, tile_expert, n_used = _moe_plan(top_i, n_experts, moe_rows)
    x_sorted = jnp.take(xn2, src_tok, axis=0, mode="clip")
    y_sorted = _moe_experts(tile_expert, n_used, x_sorted, w_gate[0], b_gate[0], w_up[0], b_up[0],
                            w_down[0], b_down[0], moe_rows)
    yg = jnp.take(y_sorted, dest.reshape(-1, TOP_K).T, axis=0, mode="clip")
    gfin = g_final.reshape(1, d)
    y_p = _final(x1_p, yg, top_w, gfin, 0, tile_rows)
    y_s = _final(x1_s, yg, top_w, gfin, np_, ns)

    return (y_p.reshape(nb, t, d), y_s.reshape(db, ds, d),
            kf_p.reshape(1, nb, t, N_HEADS, HEAD_DIM), vf_p.reshape(1, nb, t, N_HEADS, HEAD_DIM),
            kif_p.reshape(1, nb, t, IDX_DIM), conv_p[None],
            kf_s.reshape(1, db, ds, N_HEADS, HEAD_DIM), vf_s.reshape(1, db, ds, N_HEADS, HEAD_DIM),
            kif_s.reshape(1, db, ds, IDX_DIM), conv_s[None])
```
